```python
import jax, jax.numpy as jnp
from jax import lax
import numpy as np

D_MODEL = 2048
BATCH = 2
SEQ = 8192
DEPTH = 2
DEC_BATCH = 8
DEC_SEQ = 64
PAST_LEN = 1024

CHUNK = 64
D_MIX = D_MODEL
D_A = 768
D_B = 768
D_C = 512
K_A = 3
K_B = 31
POOL_WINDOWS = (2, 4, 8, 16)
N_POOL_GROUPS = len(POOL_WINDOWS)
POOL_GROUP = D_C // N_POOL_GROUPS
POOL_PAD = max(POOL_WINDOWS) - 1
D_FF = 5632
D_IN = 3 * D_A + 2 * D_B + D_C
IN_SPLITS = tuple(np.cumsum([D_A, D_A, D_A, D_B, D_B])[:].tolist())
EPS = 1e-6

kernel_name = "hybrid_streaming_conv_pool_encoder_step"


def _rms(x, g):
    x32 = x.astype(jnp.float32)
    y = x32 * lax.rsqrt(jnp.mean(x32 * x32, axis=-1, keepdims=True) + EPS)
    return (y * g.astype(jnp.float32)).astype(x.dtype)


def _layernorm(x, g, b):
    x32 = x.astype(jnp.float32)
    mu = jnp.mean(x32, axis=-1, keepdims=True)
    xc = x32 - mu
    var = jnp.mean(xc * xc, axis=-1, keepdims=True)
    y = xc * lax.rsqrt(var + EPS) * g.astype(jnp.float32) + b.astype(jnp.float32)
    return y.astype(x.dtype)


def _swiglu(x, wg, wu, wd):
    return (jax.nn.silu(x @ wg) * (x @ wu)) @ wd


def _causal_dwconv(buf, u, w):
    k, c = w.shape
    full = jnp.concatenate([buf.astype(u.dtype), u], axis=1)
    y = lax.conv_general_dilated(full, w[:, None, :].astype(u.dtype), window_strides=(1,),
                                 padding='VALID', dimension_numbers=('NWC', 'WIO', 'NWC'),
                                 feature_group_count=c)
    return y, full[:, full.shape[1] - (k - 1):]


def _pool_mixer(buf, u, pos0, pool_w, pool_scale):
    bsz, seq_len, _ = u.shape
    full = jnp.concatenate([buf.astype(u.dtype), u], axis=1)
    f32 = full.astype(jnp.float32)
    cs = jnp.concatenate([jnp.zeros((bsz, 1, D_C), jnp.float32), lax.cumsum(f32, axis=1)], axis=1)
    pos = pos0 + jnp.arange(seq_len)
    end = cs[:, POOL_PAD + 1:POOL_PAD + 1 + seq_len]
    means = []
    for g, w in enumerate(POOL_WINDOWS):
        sl = slice(g * POOL_GROUP, (g + 1) * POOL_GROUP)
        s = end[:, :, sl] - cs[:, POOL_PAD + 1 - w:POOL_PAD + 1 - w + seq_len, sl]
        cnt = jnp.minimum(pos + 1, w).astype(jnp.float32)[None, :, None]
        means.append(s / cnt)
    d = (jnp.concatenate(means, axis=-1) - u.astype(jnp.float32)).astype(u.dtype)
    d = d.reshape(bsz, seq_len, N_POOL_GROUPS, POOL_GROUP)
    y = jnp.einsum('blgc,gcd->blgd', d, pool_w).reshape(bsz, seq_len, D_C) * pool_scale
    return y, full[:, full.shape[1] - POOL_PAD:]


def _mixer(xn, buf_a, buf_b, buf_p, pos0, w_in, conv_a_w, conv_b_w, conv_b_bias,
           ln_b_gain, ln_b_bias, pool_w, pool_scale, w_out):
    z = xn @ w_in
    h_a, b_a, c_a, glu_a, glu_g, u_p = jnp.split(z, IN_SPLITS, axis=-1)
    conv_a, nbuf_a = _causal_dwconv(buf_a, c_a * h_a, conv_a_w)
    y_a = b_a * conv_a
    v = glu_a * jax.nn.sigmoid(glu_g)
    conv_b, nbuf_b = _causal_dwconv(buf_b, v, conv_b_w)
    y_b = jax.nn.silu(_layernorm(conv_b + conv_b_bias, ln_b_gain, ln_b_bias))
    y_c, nbuf_p = _pool_mixer(buf_p, u_p, pos0, pool_w, pool_scale)
    out = jnp.concatenate([y_a, y_b, y_c], axis=-1) @ w_out
    return out, nbuf_a, nbuf_b, nbuf_p


def _trunk(x, bufs_a, bufs_b, bufs_p, pos0, ffn1_norm, ffn1_wg, ffn1_wu, ffn1_wd, mix_norm, w_in,
           conv_a_w, conv_b_w, conv_b_bias, ln_b_gain, ln_b_bias, pool_w, pool_scale, w_out,
           ffn2_norm, ffn2_wg, ffn2_wu, ffn2_wd, final_norm):
    h = x
    new_a, new_b, new_p = [], [], []
    for l in range(DEPTH):
        h = h + 0.5 * _swiglu(_rms(h, ffn1_norm[l]), ffn1_wg[l], ffn1_wu[l], ffn1_wd[l])
        m, na, nb, npool = _mixer(_rms(h, mix_norm[l]), bufs_a[l], bufs_b[l], bufs_p[l], pos0,
                                  w_in[l], conv_a_w[l], conv_b_w[l], conv_b_bias[l],
                                  ln_b_gain[l], ln_b_bias[l], pool_w[l], pool_scale[l], w_out[l])
        h = h + m
        h = h + 0.5 * _swiglu(_rms(h, ffn2_norm[l]), ffn2_wg[l], ffn2_wu[l], ffn2_wd[l])
        new_a.append(na)
        new_b.append(nb)
        new_p.append(npool)
    return _rms(h, final_norm), jnp.stack(new_a), jnp.stack(new_b), jnp.stack(new_p)


def setup_inputs(seed: int = 0) -> dict:
    key = jax.random.key(seed)
    ks = jax.random.split(key, 32)
    f = jnp.float32
    nrm = lambda k, shape, s: jax.random.normal(k, shape, f) * s
    return {
        "x_prompt": nrm(ks[0], (BATCH, SEQ, D_MODEL), 1.0),
        "x_sample": nrm(ks[1], (DEC_BATCH, DEC_SEQ, D_MODEL), 1.0),
        "cache_conv_a": nrm(ks[2], (DEPTH, DEC_BATCH, K_A - 1, D_A), 1.0),
        "cache_conv_b": nrm(ks[3], (DEPTH, DEC_BATCH, K_B - 1, D_B), 1.0),
        "cache_pool": nrm(ks[4], (DEPTH, DEC_BATCH, POOL_PAD, D_C), 1.0),
        "ffn1_norm": 1.0 + nrm(ks[5], (DEPTH, D_MODEL), 0.01),
        "ffn1_wg": nrm(ks[6], (DEPTH, D_MODEL, D_FF), D_MODEL ** -0.5),
        "ffn1_wu": nrm(ks[7], (DEPTH, D_MODEL, D_FF), D_MODEL ** -0.5),
        "ffn1_wd": nrm(ks[8], (DEPTH, D_FF, D_MODEL), D_FF ** -0.5),
        "mix_norm": 1.0 + nrm(ks[9], (DEPTH, D_MODEL), 0.01),
        "w_in": nrm(ks[10], (DEPTH, D_MODEL, D_IN), D_MODEL ** -0.5),
        "conv_a_w": nrm(ks[11], (DEPTH, K_A, D_A), K_A ** -0.5),
        "conv_b_w": nrm(ks[12], (DEPTH, K_B, D_B), K_B ** -0.5),
        "conv_b_bias": nrm(ks[13], (DEPTH, D_B), 0.02),
        "ln_b_gain": 1.0 + nrm(ks[14], (DEPTH, D_B), 0.01),
        "ln_b_bias": nrm(ks[15], (DEPTH, D_B), 0.02),
        "pool_w": nrm(ks[16], (DEPTH, N_POOL_GROUPS, POOL_GROUP, POOL_GROUP), POOL_GROUP ** -0.5),
        "pool_scale": 1.0 + nrm(ks[17], (DEPTH, D_C), 0.1),
        "w_out": nrm(ks[18], (DEPTH, D_MIX, D_MODEL), D_MIX ** -0.5),
        "ffn2_norm": 1.0 + nrm(ks[19], (DEPTH, D_MODEL), 0.01),
        "ffn2_wg": nrm(ks[20], (DEPTH, D_MODEL, D_FF), D_MODEL ** -0.5),
        "ffn2_wu": nrm(ks[21], (DEPTH, D_MODEL, D_FF), D_MODEL ** -0.5),
        "ffn2_wd": nrm(ks[22], (DEPTH, D_FF, D_MODEL), D_FF ** -0.5),
        "final_norm": 1.0 + nrm(ks[23], (D_MODEL,), 0.01),
    }


def reference(x_prompt, x_sample, cache_conv_a, cache_conv_b, cache_pool, ffn1_norm, ffn1_wg,
              ffn1_wu, ffn1_wd, mix_norm, w_in, conv_a_w, conv_b_w, conv_b_bias, ln_b_gain,
              ln_b_bias, pool_w, pool_scale, w_out, ffn2_norm, ffn2_wg, ffn2_wu, ffn2_wd, final_norm):
    weights = (ffn1_norm, ffn1_wg, ffn1_wu, ffn1_wd, mix_norm, w_in, conv_a_w, conv_b_w, conv_b_bias,
               ln_b_gain, ln_b_bias, pool_w, pool_scale, w_out, ffn2_norm, ffn2_wg, ffn2_wu, ffn2_wd,
               final_norm)
    bp = x_prompt.shape[0]
    dt = x_prompt.dtype
    zero_a = jnp.zeros((DEPTH, bp, K_A - 1, D_A), dt)
    zero_b = jnp.zeros((DEPTH, bp, K_B - 1, D_B), dt)
    zero_p = jnp.zeros((DEPTH, bp, POOL_PAD, D_C), dt)
    y_prompt, new_a_p, new_b_p, new_pool_p = _trunk(x_prompt, zero_a, zero_b, zero_p, 0, *weights)
    y_sample, new_a_s, new_b_s, new_pool_s = _trunk(x_sample, cache_conv_a, cache_conv_b, cache_pool,
                                                     PAST_LEN, *weights)
    return (y_prompt, y_sample, new_a_p, new_b_p, new_pool_p, new_a_s, new_b_s, new_pool_s)
```

```python
import functools

import jax
import jax.numpy as jnp
from jax import lax
from jax.experimental import pallas as pl
from jax.experimental.pallas import tpu as pltpu

EPS = 1e-6
PAST_LEN = 1024
D_A = 768
D_B = 768
D_C = 512
K_A = 3
K_B = 31
POOL_WINDOWS = (2, 4, 8, 16)
POOL_GROUP = D_C // len(POOL_WINDOWS)
POOL_PAD = max(POOL_WINDOWS) - 1

LANE = 128
HIST_A = 8
HIST_B = 32
HIST_C = 16
ROW_CHUNK = 32

VMEM_LIMIT_BYTES = 60 * 1024 * 1024

F32 = jnp.float32
BF16 = jnp.bfloat16


def _rms_rows(x, w):
    ms = jnp.mean(x * x, axis=-1, keepdims=True)
    return (x * lax.rsqrt(ms + EPS)) * w


def _ffn_body(x_ref, nw_ref, wg_ref, wu_ref, wd_ref, fw_ref, o_ref, xn_ref, *, final_norm):
    f = pl.program_id(1)

    @pl.when(f == 0)
    def _():
        x = x_ref[...]
        xn_ref[...] = _rms_rows(x, nw_ref[...]).astype(BF16)
        o_ref[...] = x

    xn = xn_ref[...]
    g = jnp.dot(xn, wg_ref[...], preferred_element_type=F32)
    u = jnp.dot(xn, wu_ref[...], preferred_element_type=F32)
    a = ((0.5 * g) * jax.nn.sigmoid(g)) * u
    o_ref[...] += jnp.dot(a.astype(BF16), wd_ref[...], preferred_element_type=F32)

    if final_norm:
        @pl.when(f == pl.num_programs(1) - 1)
        def _():
            o_ref[...] = _rms_rows(o_ref[...], fw_ref[...])


def _ffn(x, norm_w, wg, wu, wd, layer, final_w, *, tm, tf):
    m, d = x.shape
    d_ff = wg.shape[-1]
    assert m % tm == 0 and d_ff % tf == 0
    final_norm = final_w is not None
    fw = final_w.reshape(1, d) if final_norm else norm_w[layer].reshape(1, d)
    return pl.pallas_call(
        functools.partial(_ffn_body, final_norm=final_norm),
        out_shape=jax.ShapeDtypeStruct((m, d), F32),
        grid=(m // tm, d_ff // tf),
        in_specs=[
            pl.BlockSpec((tm, d), lambda i, f: (i, 0)),
            pl.BlockSpec((None, 1, d), lambda i, f: (layer, 0, 0)),
            pl.BlockSpec((None, d, tf), lambda i, f: (layer, 0, f)),
            pl.BlockSpec((None, d, tf), lambda i, f: (layer, 0, f)),
            pl.BlockSpec((None, tf, d), lambda i, f: (layer, f, 0)),
            pl.BlockSpec((1, d), lambda i, f: (0, 0)),
        ],
        out_specs=pl.BlockSpec((tm, d), lambda i, f: (i, 0)),
        scratch_shapes=[pltpu.VMEM((tm, d), BF16)],
        compiler_params=pltpu.CompilerParams(
            dimension_semantics=("arbitrary", "arbitrary"),
            vmem_limit_bytes=VMEM_LIMIT_BYTES),
        name="ffn",
    )(x, norm_w.reshape(norm_w.shape[0], 1, d), wg, wu, wd, fw)


def _mixer_body(*refs, nseg, seg_len, tiles_per_seq, pos0, has_cache):
    it = iter(refs)
    h_ref, nw_ref, win_ref, caw_ref, cbw_ref, cbb_ref, lng_ref, lnb_ref, pw_ref, ps_ref, wout_ref = (
        next(it) for _ in range(11))
    if has_cache:
        ca_ref, cb_ref, cp_ref = (next(it) for _ in range(3))
    o_ref, na_ref, nb_ref, np_ref = (next(it) for _ in range(4))
    full_a, full_b, full_c, zbuf, ybuf, dbuf = (next(it) for _ in range(6))

    L = seg_len
    CH = ROW_CHUNK
    n_chunks = L // CH
    j = pl.program_id(0) % tiles_per_seq
    slab = lambda c: slice(c * LANE, (c + 1) * LANE)
    hist_bufs = ((full_a, HIST_A, D_A), (full_b, HIST_B, D_B), (full_c, HIST_C, D_C))

    def init_history():
        cache_refs = (ca_ref, cb_ref, cp_ref) if has_cache else (None,) * 3
        for (buf, hist, width), cache in zip(hist_bufs, cache_refs):
            for s in range(nseg):
                for c in range(width // LANE):
                    buf[s, c, 0:hist, :] = cache[s, :, slab(c)] if has_cache else jnp.zeros((hist, LANE), F32)

    def carry_history():
        for buf, hist, width in hist_bufs:
            for s in range(nseg):
                for c in range(width // LANE):
                    buf[s, c, 0:hist, :] = buf[s, c, L:L + hist, :]

    if tiles_per_seq == 1:
        init_history()
    else:
        pl.when(j == 0)(init_history)
        pl.when(j != 0)(carry_history)

    xn = _rms_rows(h_ref[...], nw_ref[...]).astype(BF16)

    def for_chunks(body):
        for s in range(nseg):
            if n_chunks <= 2:
                for c in range(n_chunks):
                    body(s, c * CH)
            else:
                def step(c, carry, s=s):
                    body(s, pl.multiple_of(c * CH, CH))
                    return carry
                lax.fori_loop(0, n_chunks, step, 0)

    def tap(buf, s, c, row0):
        return buf[s, c, pl.ds(row0, CH, stride=1), :]

    zbuf[:, 0:3 * D_A] = jnp.dot(xn, win_ref[:, 0:3 * D_A], preferred_element_type=F32)

    def stage_a(s, t0):
        rows = pl.ds(s * L + t0, CH)
        for c in range(D_A // LANE):
            p = zbuf[rows, 2 * D_A + c * LANE:2 * D_A + (c + 1) * LANE] * zbuf[rows, slab(c)]
            full_a[s, c, pl.ds(HIST_A + t0, CH), :] = p
            conv = caw_ref[K_A - 1:K_A, slab(c)] * p
            for k in range(K_A - 1):
                conv = conv + caw_ref[k:k + 1, slab(c)] * tap(full_a, s, c, HIST_A - (K_A - 1) + k + t0)
            b_gate = zbuf[rows, D_A + c * LANE:D_A + (c + 1) * LANE]
            ybuf[rows, slab(c)] = (b_gate * conv).astype(BF16)

    for_chunks(stage_a)

    b0 = 3 * D_A
    zbuf[:, 0:2 * D_B] = jnp.dot(xn, win_ref[:, b0:b0 + 2 * D_B], preferred_element_type=F32)

    def stage_b(s, t0):
        rows = pl.ds(s * L + t0, CH)
        n_slab = D_B // LANE
        xs = []
        for c in range(n_slab):
            v = zbuf[rows, slab(c)] * jax.nn.sigmoid(zbuf[rows, D_B + c * LANE:D_B + (c + 1) * LANE])
            full_b[s, c, pl.ds(HIST_B + t0, CH), :] = v
            conv = cbw_ref[K_B - 1:K_B, slab(c)] * v
            for k in range(K_B - 1):
                conv = conv + cbw_ref[k:k + 1, slab(c)] * tap(full_b, s, c, HIST_B - (K_B - 1) + k + t0)
            xs.append(conv + cbb_ref[:, slab(c)])
        mu = jnp.sum(functools.reduce(jnp.add, xs), axis=-1, keepdims=True) * (1.0 / D_B)
        xc = [x - mu for x in xs]
        var = jnp.sum(functools.reduce(jnp.add, [x * x for x in xc]), axis=-1, keepdims=True) * (1.0 / D_B)
        inv = lax.rsqrt(var + EPS)
        for c in range(n_slab):
            y = xc[c] * inv * lng_ref[:, slab(c)] + lnb_ref[:, slab(c)]
            ybuf[rows, D_A + c * LANE:D_A + (c + 1) * LANE] = (y * jax.nn.sigmoid(y)).astype(BF16)

    for_chunks(stage_b)

    c0 = 3 * D_A + 2 * D_B
    zbuf[:, 0:D_C] = jnp.dot(xn, win_ref[:, c0:c0 + D_C], preferred_element_type=F32)

    def stage_c(s, t0):
        rows = pl.ds(s * L + t0, CH)
        pos = pos0 + j * L + t0 + lax.broadcasted_iota(jnp.int32, (CH, POOL_GROUP), 0)
        for g, w in enumerate(POOL_WINDOWS):
            u = zbuf[rows, slab(g)]
            full_c[s, g, pl.ds(HIST_C + t0, CH), :] = u
            acc = u
            for back in range(1, w):
                acc = acc + tap(full_c, s, g, HIST_C - back + t0)
            cnt = jnp.minimum(pos + 1, w).astype(F32)
            dbuf[rows, slab(g)] = (acc / cnt - u).astype(BF16)

    for_chunks(stage_c)

    y0 = D_A + D_B
    for g in range(len(POOL_WINDOWS)):
        yc = jnp.dot(dbuf[:, slab(g)], pw_ref[g], preferred_element_type=F32) * ps_ref[:, slab(g)]
        ybuf[:, y0 + g * POOL_GROUP:y0 + (g + 1) * POOL_GROUP] = yc.astype(BF16)

    o_ref[...] = h_ref[...] + jnp.dot(ybuf[...], wout_ref[...], preferred_element_type=F32)

    for (buf, hist, width), out in zip(hist_bufs, (na_ref, nb_ref, np_ref)):
        for s in range(nseg):
            for c in range(width // LANE):
                out[s, :, slab(c)] = buf[s, c, L:L + hist, :]


def _mixer(h, layer, n_seq, pos0, caches, mix_norm, w_in, conv_a_w, conv_b_w, conv_b_bias, ln_g, ln_b,
           pool_w, pool_scale, w_out, *, tm):
    m, d = h.shape
    seq_len = m // n_seq
    if seq_len >= tm:
        nseg, seg_len, tiles_per_seq = 1, tm, seq_len // tm
        assert seq_len % tm == 0
    else:
        nseg, seg_len, tiles_per_seq = tm // seq_len, seq_len, 1
        assert tm % seq_len == 0 and n_seq % nseg == 0
    assert seg_len % ROW_CHUNK == 0 and seg_len >= HIST_B
    has_cache = caches is not None
    depth = w_in.shape[0]
    d_in = w_in.shape[-1]

    def const(shape):
        nd = len(shape)
        return pl.BlockSpec((None,) + shape, lambda i: (layer,) + (0,) * nd, pipeline_mode=pl.Buffered(1))

    in_specs = [
        pl.BlockSpec((tm, d), lambda i: (i, 0)),
        const((1, d)),
        const((d, d_in)),
        const((K_A, D_A)),
        const((K_B, D_B)),
        const((1, D_B)),
        const((1, D_B)),
        const((1, D_B)),
        const((len(POOL_WINDOWS), POOL_GROUP, POOL_GROUP)),
        const((1, D_C)),
        const((w_out.shape[1], d)),
    ]
    args = [h, mix_norm.reshape(depth, 1, d), w_in, conv_a_w, conv_b_w, conv_b_bias.reshape(depth, 1, D_B),
            ln_g.reshape(depth, 1, D_B), ln_b.reshape(depth, 1, D_B), pool_w,
            pool_scale.reshape(depth, 1, D_C), w_out]
    if has_cache:
        ca, cb, cp = caches
        for c, hist in ((ca, HIST_A), (cb, HIST_B), (cp, HIST_C)):
            in_specs.append(pl.BlockSpec((None, nseg, hist, c.shape[-1]), lambda i: (layer, i, 0, 0)))
            args.append(c)

    seq_of = lambda i: (i // tiles_per_seq, 0, 0)
    out_shape = [
        jax.ShapeDtypeStruct((m, d), F32),
        jax.ShapeDtypeStruct((n_seq, HIST_A, D_A), F32),
        jax.ShapeDtypeStruct((n_seq, HIST_B, D_B), F32),
        jax.ShapeDtypeStruct((n_seq, HIST_C, D_C), F32),
    ]
    out_specs = [
        pl.BlockSpec((tm, d), lambda i: (i, 0)),
        pl.BlockSpec((nseg, HIST_A, D_A), seq_of),
        pl.BlockSpec((nseg, HIST_B, D_B), seq_of),
        pl.BlockSpec((nseg, HIST_C, D_C), seq_of),
    ]
    scratch = [
        pltpu.VMEM((nseg, D_A // LANE, HIST_A + seg_len, LANE), F32),
        pltpu.VMEM((nseg, D_B // LANE, HIST_B + seg_len, LANE), F32),
        pltpu.VMEM((nseg, D_C // LANE, HIST_C + seg_len, LANE), F32),
        pltpu.VMEM((tm, 3 * D_A), F32),
        pltpu.VMEM((tm, d), BF16),
        pltpu.VMEM((tm, D_C), BF16),
    ]
    return pl.pallas_call(
        functools.partial(_mixer_body, nseg=nseg, seg_len=seg_len, tiles_per_seq=tiles_per_seq,
                          pos0=pos0, has_cache=has_cache),
        out_shape=out_shape,
        grid=(m // tm,),
        in_specs=in_specs,
        out_specs=out_specs,
        scratch_shapes=scratch,
        compiler_params=pltpu.CompilerParams(
            dimension_semantics=("arbitrary",),
            vmem_limit_bytes=VMEM_LIMIT_BYTES),
        name="mixer",
    )(*args)


def _front_pad(cache, hist):
    pad = hist - cache.shape[2]
    return jnp.pad(cache, ((0, 0), (0, 0), (pad, 0), (0, 0)))


def _trunk(x, pos0, caches, w, *, tm_ffn, tf, tm_mix):
    n_seq, seq_len, d = x.shape
    h = x.reshape(n_seq * seq_len, d)
    depth = w["w_in"].shape[0]
    new_a, new_b, new_p = [], [], []
    for l in range(depth):
        h = _ffn(h, w["ffn1_norm"], w["ffn1_wg"], w["ffn1_wu"], w["ffn1_wd"], l, None, tm=tm_ffn, tf=tf)
        h, na, nb, npool = _mixer(h, l, n_seq, pos0, caches, w["mix_norm"], w["w_in"], w["conv_a_w"],
                                  w["conv_b_w"], w["conv_b_bias"], w["ln_b_gain"], w["ln_b_bias"],
                                  w["pool_w"], w["pool_scale"], w["w_out"], tm=tm_mix)
        final_w = w["final_norm"] if l == depth - 1 else None
        h = _ffn(h, w["ffn2_norm"], w["ffn2_wg"], w["ffn2_wu"], w["ffn2_wd"], l, final_w, tm=tm_ffn, tf=tf)
        new_a.append(na[:, HIST_A - (K_A - 1):])
        new_b.append(nb[:, HIST_B - (K_B - 1):])
        new_p.append(npool[:, HIST_C - POOL_PAD:])
    return h.reshape(n_seq, seq_len, d), jnp.stack(new_a), jnp.stack(new_b), jnp.stack(new_p)


def kernel(x_prompt, x_sample, cache_conv_a, cache_conv_b, cache_pool, ffn1_norm, ffn1_wg, ffn1_wu, ffn1_wd,
           mix_norm, w_in, conv_a_w, conv_b_w, conv_b_bias, ln_b_gain, ln_b_bias, pool_w, pool_scale, w_out,
           ffn2_norm, ffn2_wg, ffn2_wu, ffn2_wd, final_norm):
    w = dict(
        ffn1_norm=ffn1_norm, ffn1_wg=ffn1_wg.astype(BF16), ffn1_wu=ffn1_wu.astype(BF16),
        ffn1_wd=ffn1_wd.astype(BF16), mix_norm=mix_norm, w_in=w_in.astype(BF16), conv_a_w=conv_a_w,
        conv_b_w=conv_b_w, conv_b_bias=conv_b_bias, ln_b_gain=ln_b_gain, ln_b_bias=ln_b_bias,
        pool_w=pool_w.astype(BF16), pool_scale=pool_scale, w_out=w_out.astype(BF16),
        ffn2_norm=ffn2_norm, ffn2_wg=ffn2_wg.astype(BF16), ffn2_wu=ffn2_wu.astype(BF16),
        ffn2_wd=ffn2_wd.astype(BF16), final_norm=final_norm)

    n_dec = x_sample.shape[0] * x_sample.shape[1]
    y_p, a_p, b_p, p_p = _trunk(x_prompt, 0, None, w, tm_ffn=512, tf=512, tm_mix=512)
    caches = (_front_pad(cache_conv_a, HIST_A), _front_pad(cache_conv_b, HIST_B),
              _front_pad(cache_pool, HIST_C))
    y_s, a_s, b_s, p_s = _trunk(x_sample, PAST_LEN, caches, w, tm_ffn=n_dec, tf=512, tm_mix=n_dec)
    return (y_p, y_s, a_p, b_p, p_p, a_s, b_s, p_s)
```

```python
import functools

import jax
import jax.numpy as jnp
from jax import lax
from jax.experimental import pallas as pl
from jax.experimental.pallas import tpu as pltpu

EPS = 1e-6
PAST_LEN = 1024
D_A = 768
D_B = 768
D_C = 512
K_A = 3
K_B = 31
POOL_WINDOWS = (2, 4, 8, 16)
POOL_GROUP = D_C // len(POOL_WINDOWS)
POOL_PAD = max(POOL_WINDOWS) - 1

LANE = 128
SUBLANE = 8
COL_BLOCK = 256
HIST_A = 8
HIST_B = 32
HIST_C = 16
ROW_CHUNK = 32

VMEM_LIMIT_BYTES = 60 * 1024 * 1024

F32 = jnp.float32
BF16 = jnp.bfloat16


def _rms_rows(x, w):
    ms = jnp.mean(x * x, axis=-1, keepdims=True)
    return (x * lax.rsqrt(ms + EPS)) * w


def _ffn_body(x_ref, nw_ref, wg_ref, wu_ref, wd_ref, fw_ref, o_ref, xn_ref, *, final_norm):
    f = pl.program_id(1)

    @pl.when(f == 0)
    def _():
        x = x_ref[...]
        xn_ref[...] = _rms_rows(x, nw_ref[...]).astype(BF16)
        o_ref[...] = x

    xn = xn_ref[...]
    g = jnp.dot(xn, wg_ref[...], preferred_element_type=F32)
    u = jnp.dot(xn, wu_ref[...], preferred_element_type=F32)
    a = ((0.5 * g) * jax.nn.sigmoid(g)) * u
    o_ref[...] += jnp.dot(a.astype(BF16), wd_ref[...], preferred_element_type=F32)

    if final_norm:
        @pl.when(f == pl.num_programs(1) - 1)
        def _():
            o_ref[...] = _rms_rows(o_ref[...], fw_ref[...])


def _ffn(x, norm_w, wg, wu, wd, layer, final_w, *, tm, tf):
    m, d = x.shape
    d_ff = wg.shape[-1]
    assert m % tm == 0 and d_ff % tf == 0
    final_norm = final_w is not None
    fw = final_w.reshape(1, d) if final_norm else norm_w[layer].reshape(1, d)
    return pl.pallas_call(
        functools.partial(_ffn_body, final_norm=final_norm),
        out_shape=jax.ShapeDtypeStruct((m, d), F32),
        grid=(m // tm, d_ff // tf),
        in_specs=[
            pl.BlockSpec((tm, d), lambda i, f: (i, 0)),
            pl.BlockSpec((None, 1, d), lambda i, f: (layer, 0, 0)),
            pl.BlockSpec((None, d, tf), lambda i, f: (layer, 0, f)),
            pl.BlockSpec((None, d, tf), lambda i, f: (layer, 0, f)),
            pl.BlockSpec((None, tf, d), lambda i, f: (layer, f, 0)),
            pl.BlockSpec((1, d), lambda i, f: (0, 0)),
        ],
        out_specs=pl.BlockSpec((tm, d), lambda i, f: (i, 0)),
        scratch_shapes=[pltpu.VMEM((tm, d), BF16)],
        compiler_params=pltpu.CompilerParams(
            dimension_semantics=("arbitrary", "arbitrary"),
            vmem_limit_bytes=VMEM_LIMIT_BYTES),
        name="ffn",
    )(x, norm_w.reshape(norm_w.shape[0], 1, d), wg, wu, wd, fw)


def _mixer_body(*refs, nseg, seg_len, tiles_per_seq, pos0, has_cache):
    it = iter(refs)
    (hc_ref, hp_ref, nw_ref, w3_ref, caw_ref, cbw_ref, cbb_ref, lng_ref, lnb_ref, pw_ref, ps_ref,
     wout_ref) = (next(it) for _ in range(12))
    if has_cache:
        ca_ref, cb_ref, cp_ref = (next(it) for _ in range(3))
    o_ref, na_ref, nb_ref, np_ref = (next(it) for _ in range(4))
    full_a, full_b, full_c, z_even, z_odd, xn_ref, ybuf, dbuf = (next(it) for _ in range(8))

    L = seg_len
    CH = ROW_CHUNK
    NG = CH // SUBLANE
    tm = nseg * L
    n_chunks = tm // CH
    chunks_per_seg = L // CH
    n_cb = w3_ref.shape[0]
    cb_per_iter = n_cb // n_chunks
    i = pl.program_id(0)
    je = (i + tiles_per_seq - 1) % tiles_per_seq
    slab = lambda c: slice(c * LANE, (c + 1) * LANE)
    hist_bufs = ((full_a, HIST_A, D_A), (full_b, HIST_B, D_B), (full_c, HIST_C, D_C))

    @pl.when(i == 0)
    def _():
        z_odd[...] = jnp.zeros(z_odd.shape, F32)

    def init_history():
        cache_refs = (ca_ref, cb_ref, cp_ref) if has_cache else (None,) * 3
        for (buf, hist, width), cache in zip(hist_bufs, cache_refs):
            for s in range(nseg):
                for c in range(width // LANE):
                    buf[s, c, 0:hist, :] = cache[s, :, slab(c)] if has_cache else jnp.zeros((hist, LANE), F32)

    def carry_history():
        for buf, hist, width in hist_bufs:
            for s in range(nseg):
                for c in range(width // LANE):
                    buf[s, c, 0:hist, :] = buf[s, c, L:L + hist, :]

    if tiles_per_seq == 1:
        init_history()
    else:
        first = jnp.logical_or(je == 0, i == 0)
        pl.when(first)(init_history)
        pl.when(jnp.logical_not(first))(carry_history)

    xn_ref[...] = _rms_rows(hc_ref[...], nw_ref[...]).astype(BF16)

    def run(z_write, z_read):
        def project(cb):
            z_write[cb] = jnp.dot(xn_ref[...], w3_ref[cb], preferred_element_type=F32)

        def z(col, rows):
            return z_read[col // COL_BLOCK, rows, col % COL_BLOCK:col % COL_BLOCK + LANE]

        def tap(buf, s, c, row0):
            return buf[s, c, pl.ds(row0, SUBLANE, stride=1), :]

        def dwconv(buf, w_ref, n_taps, hist, s, c, t0, newest):
            accs = [None] * NG
            for k in range(n_taps):
                wk = jnp.broadcast_to(w_ref[k:k + 1, slab(c)], (SUBLANE, LANE))
                for r in range(NG):
                    if k == n_taps - 1:
                        x = newest[r * SUBLANE:(r + 1) * SUBLANE]
                    else:
                        x = tap(buf, s, c, hist - (n_taps - 1) + k + t0 + r * SUBLANE)
                    accs[r] = wk * x if accs[r] is None else accs[r] + wk * x
            return jnp.concatenate(accs, axis=0)

        def mixers(s, t0, rows):
            for c in range(D_A // LANE):
                pa = z(2 * D_A + c * LANE, rows) * z(c * LANE, rows)
                full_a[s, c, pl.ds(HIST_A + t0, CH), :] = pa
                conv = dwconv(full_a, caw_ref, K_A, HIST_A, s, c, t0, pa)
                ybuf[rows, slab(c)] = (z(D_A + c * LANE, rows) * conv).astype(BF16)

            b0 = 3 * D_A
            xs = []
            for c in range(D_B // LANE):
                v = z(b0 + c * LANE, rows) * jax.nn.sigmoid(z(b0 + D_B + c * LANE, rows))
                full_b[s, c, pl.ds(HIST_B + t0, CH), :] = v
                xs.append(dwconv(full_b, cbw_ref, K_B, HIST_B, s, c, t0, v) + cbb_ref[:, slab(c)])
            mu = jnp.sum(functools.reduce(jnp.add, xs), axis=-1, keepdims=True) * (1.0 / D_B)
            xc = [x - mu for x in xs]
            var = jnp.sum(functools.reduce(jnp.add, [x * x for x in xc]), axis=-1, keepdims=True) * (1.0 / D_B)
            inv = lax.rsqrt(var + EPS)
            for c in range(D_B // LANE):
                y = xc[c] * inv * lng_ref[:, slab(c)] + lnb_ref[:, slab(c)]
                ybuf[rows, D_A + c * LANE:D_A + (c + 1) * LANE] = (y * jax.nn.sigmoid(y)).astype(BF16)

            c0 = 3 * D_A + 2 * D_B
            pos = pos0 + je * L + t0 + lax.broadcasted_iota(jnp.int32, (CH, POOL_GROUP), 0)
            for g, w in enumerate(POOL_WINDOWS):
                u = z(c0 + g * LANE, rows)
                full_c[s, g, pl.ds(HIST_C + t0, CH), :] = u
                accs = [u[r * SUBLANE:(r + 1) * SUBLANE] for r in range(NG)]
                for back in range(1, w):
                    for r in range(NG):
                        accs[r] = accs[r] + tap(full_c, s, g, HIST_C - back + t0 + r * SUBLANE)
                cnt = jnp.minimum(pos + 1, w).astype(F32)
                dbuf[rows, slab(g)] = (jnp.concatenate(accs, axis=0) / cnt - u).astype(BF16)

        def step(it_, carry):
            row0 = pl.multiple_of(it_ * CH, CH)
            if nseg == 1:
                s, t0 = 0, row0
            else:
                s = it_ // chunks_per_seg
                t0 = pl.multiple_of((it_ % chunks_per_seg) * CH, CH)
            mixers(s, t0, pl.ds(row0, CH))
            for b in range(cb_per_iter):
                project(it_ * cb_per_iter + b)
            return carry

        lax.fori_loop(0, n_chunks, step, 0)
        for cb in range(cb_per_iter * n_chunks, n_cb):
            project(cb)

    pl.when(i % 2 == 0)(lambda: run(z_even, z_odd))
    pl.when(i % 2 == 1)(lambda: run(z_odd, z_even))

    y0 = D_A + D_B
    for g in range(len(POOL_WINDOWS)):
        yc = jnp.dot(dbuf[:, slab(g)], pw_ref[g], preferred_element_type=F32) * ps_ref[:, slab(g)]
        ybuf[:, y0 + g * POOL_GROUP:y0 + (g + 1) * POOL_GROUP] = yc.astype(BF16)

    o_ref[...] = hp_ref[...] + jnp.dot(ybuf[...], wout_ref[...], preferred_element_type=F32)

    for (buf, hist, width), out in zip(hist_bufs, (na_ref, nb_ref, np_ref)):
        for s in range(nseg):
            for c in range(width // LANE):
                out[s, :, slab(c)] = buf[s, c, L:L + hist, :]


def _mixer(h, layer, n_seq, pos0, caches, mix_norm, w_in3, conv_a_w, conv_b_w, conv_b_bias, ln_g, ln_b,
           pool_w, pool_scale, w_out, *, tm):
    m, d = h.shape
    seq_len = m // n_seq
    if seq_len >= tm:
        nseg, seg_len, tiles_per_seq = 1, tm, seq_len // tm
        assert seq_len % tm == 0
    else:
        nseg, seg_len, tiles_per_seq = tm // seq_len, seq_len, 1
        assert tm % seq_len == 0 and n_seq % nseg == 0
    assert seg_len % ROW_CHUNK == 0 and seg_len >= HIST_B
    has_cache = caches is not None
    depth, n_cb = w_in3.shape[:2]
    n_tiles = m // tm
    assert n_cb >= tm // ROW_CHUNK

    cur = lambda i: jnp.minimum(i, n_tiles - 1)
    prev = lambda i: jnp.maximum(i - 1, 0)

    def const(shape):
        nd = len(shape)
        return pl.BlockSpec((None,) + shape, lambda i: (layer,) + (0,) * nd, pipeline_mode=pl.Buffered(1))

    in_specs = [
        pl.BlockSpec((tm, d), lambda i: (cur(i), 0)),
        pl.BlockSpec((tm, d), lambda i: (prev(i), 0)),
        const((1, d)),
        const((n_cb, d, COL_BLOCK)),
        const((K_A, D_A)),
        const((K_B, D_B)),
        const((1, D_B)),
        const((1, D_B)),
        const((1, D_B)),
        const((len(POOL_WINDOWS), POOL_GROUP, POOL_GROUP)),
        const((1, D_C)),
        const((w_out.shape[1], d)),
    ]
    args = [h, h, mix_norm.reshape(depth, 1, d), w_in3, conv_a_w, conv_b_w, conv_b_bias.reshape(depth, 1, D_B),
            ln_g.reshape(depth, 1, D_B), ln_b.reshape(depth, 1, D_B), pool_w,
            pool_scale.reshape(depth, 1, D_C), w_out]
    if has_cache:
        ca, cb, cp = caches
        for c, hist in ((ca, HIST_A), (cb, HIST_B), (cp, HIST_C)):
            in_specs.append(pl.BlockSpec((None, nseg, hist, c.shape[-1]), lambda i: (layer, prev(i), 0, 0)))
            args.append(c)

    seq_of = lambda i: (prev(i) // tiles_per_seq, 0, 0)
    out_shape = [
        jax.ShapeDtypeStruct((m, d), F32),
        jax.ShapeDtypeStruct((n_seq, HIST_A, D_A), F32),
        jax.ShapeDtypeStruct((n_seq, HIST_B, D_B), F32),
        jax.ShapeDtypeStruct((n_seq, HIST_C, D_C), F32),
    ]
    out_specs = [
        pl.BlockSpec((tm, d), lambda i: (prev(i), 0)),
        pl.BlockSpec((nseg, HIST_A, D_A), seq_of),
        pl.BlockSpec((nseg, HIST_B, D_B), seq_of),
        pl.BlockSpec((nseg, HIST_C, D_C), seq_of),
    ]
    scratch = [
        pltpu.VMEM((nseg, D_A // LANE, HIST_A + seg_len, LANE), F32),
        pltpu.VMEM((nseg, D_B // LANE, HIST_B + seg_len, LANE), F32),
        pltpu.VMEM((nseg, D_C // LANE, HIST_C + seg_len, LANE), F32),
        pltpu.VMEM((n_cb, tm, COL_BLOCK), F32),
        pltpu.VMEM((n_cb, tm, COL_BLOCK), F32),
        pltpu.VMEM((tm, d), BF16),
        pltpu.VMEM((tm, d), BF16),
        pltpu.VMEM((tm, D_C), BF16),
    ]
    return pl.pallas_call(
        functools.partial(_mixer_body, nseg=nseg, seg_len=seg_len, tiles_per_seq=tiles_per_seq,
                          pos0=pos0, has_cache=has_cache),
        out_shape=out_shape,
        grid=(n_tiles + 1,),
        in_specs=in_specs,
        out_specs=out_specs,
        scratch_shapes=scratch,
        compiler_params=pltpu.CompilerParams(
            dimension_semantics=("arbitrary",),
            vmem_limit_bytes=VMEM_LIMIT_BYTES),
        name="mixer",
    )(*args)


def _front_pad(cache, hist):
    pad = hist - cache.shape[2]
    return jnp.pad(cache, ((0, 0), (0, 0), (pad, 0), (0, 0)))


def _col_blocks(w_in):
    depth, d, d_in = w_in.shape
    return w_in.astype(BF16).reshape(depth, d, d_in // COL_BLOCK, COL_BLOCK).transpose(0, 2, 1, 3)


def _trunk(x, pos0, caches, w, *, tm_ffn, tf, tm_mix):
    n_seq, seq_len, d = x.shape
    h = x.reshape(n_seq * seq_len, d)
    depth = w["w_in3"].shape[0]
    new_a, new_b, new_p = [], [], []
    for l in range(depth):
        h = _ffn(h, w["ffn1_norm"], w["ffn1_wg"], w["ffn1_wu"], w["ffn1_wd"], l, None, tm=tm_ffn, tf=tf)
        h, na, nb, npool = _mixer(h, l, n_seq, pos0, caches, w["mix_norm"], w["w_in3"], w["conv_a_w"],
                                  w["conv_b_w"], w["conv_b_bias"], w["ln_b_gain"], w["ln_b_bias"],
                                  w["pool_w"], w["pool_scale"], w["w_out"], tm=tm_mix)
        final_w = w["final_norm"] if l == depth - 1 else None
        h = _ffn(h, w["ffn2_norm"], w["ffn2_wg"], w["ffn2_wu"], w["ffn2_wd"], l, final_w, tm=tm_ffn, tf=tf)
        new_a.append(na[:, HIST_A - (K_A - 1):])
        new_b.append(nb[:, HIST_B - (K_B - 1):])
        new_p.append(npool[:, HIST_C - POOL_PAD:])
    return h.reshape(n_seq, seq_len, d), jnp.stack(new_a), jnp.stack(new_b), jnp.stack(new_p)


def kernel(x_prompt, x_sample, cache_conv_a, cache_conv_b, cache_pool, ffn1_norm, ffn1_wg, ffn1_wu, ffn1_wd,
           mix_norm, w_in, conv_a_w, conv_b_w, conv_b_bias, ln_b_gain, ln_b_bias, pool_w, pool_scale, w_out,
           ffn2_norm, ffn2_wg, ffn2_wu, ffn2_wd, final_norm):
    w = dict(
        ffn1_norm=ffn1_norm, ffn1_wg=ffn1_wg.astype(BF16), ffn1_wu=ffn1_wu.astype(BF16),
        ffn1_wd=ffn1_wd.astype(BF16), mix_norm=mix_norm, w_in3=_col_blocks(w_in), conv_a_w=conv_a_w,
        conv_b_w=conv_b_w, conv_b_bias=conv_b_bias, ln_b_gain=ln_b_gain, ln_b_bias=ln_b_bias,
        pool_w=pool_w.astype(BF16), pool_scale=pool_scale, w_out=w_out.astype(BF16),
        ffn2_norm=ffn2_norm, ffn2_wg=ffn2_wg.astype(BF16), ffn2_wu=ffn2_wu.astype(BF16),
        ffn2_wd=ffn2_wd.astype(BF16), final_norm=final_norm)

    n_dec = x_sample.shape[0] * x_sample.shape[1]
    y_p, a_p, b_p, p_p = _trunk(x_prompt, 0, None, w, tm_ffn=512, tf=512, tm_mix=256)
    caches = (_front_pad(cache_conv_a, HIST_A), _front_pad(cache_conv_b, HIST_B),
              _front_pad(cache_pool, HIST_C))
    y_s, a_s, b_s, p_s = _trunk(x_sample, PAST_LEN, caches, w, tm_ffn=n_dec, tf=512, tm_mix=256)
    return (y_p, y_s, a_p, b_p, p_p, a_s, b_s, p_s)
```

```python
import functools

import jax
import jax.numpy as jnp
from jax import lax
from jax.experimental import pallas as pl
from jax.experimental.pallas import tpu as pltpu

EPS = 1e-6
PAST_LEN = 1024
D_MODEL = 2048
D_A = 768
D_B = 768
D_C = 512
K_A = 3
K_B = 31
POOL_WINDOWS = (2, 4, 8, 16)
POOL_GROUP = D_C // len(POOL_WINDOWS)
POOL_PAD = max(POOL_WINDOWS) - 1

LANE = 128
SUBLANE = 8
COL_BLOCK = 256
HIST_A = 32
HIST_B = 32
HIST_C = 16
ROW_CHUNK = 32
MIXER_PARAM_LAYOUT = {
    "mix_norm": (0, 0, D_MODEL),
    "conv_a_w0": (1, 0, D_A), "conv_a_w1": (1, D_A, D_A),
    "conv_a_w2": (2, 0, D_A), "conv_b_bias": (2, D_A, D_B),
    "ln_b_gain": (3, 0, D_B), "ln_b_bias": (3, D_B, D_B),
    "pool_scale": (4, 0, D_C),
}

VMEM_LIMIT_BYTES = 60 * 1024 * 1024

F32 = jnp.float32
BF16 = jnp.bfloat16


def _rms_rows(x, w):
    ms = jnp.mean(x * x, axis=-1, keepdims=True)
    return (x * lax.rsqrt(ms + EPS)) * w


def _ffn_body(x_ref, wg_ref, wu_ref, wd_ref, nw_ref, o_ref, xn_ref, *, final_norm):
    f = pl.program_id(1)

    @pl.when(f == 0)
    def _():
        x = x_ref[...]
        xn_ref[...] = _rms_rows(x, nw_ref[0:1, :]).astype(BF16)
        o_ref[...] = x

    xn = xn_ref[...]
    g = jnp.dot(xn, wg_ref[...], preferred_element_type=F32)
    u = jnp.dot(xn, wu_ref[...], preferred_element_type=F32)
    a = ((0.5 * g) * jax.nn.sigmoid(g)) * u
    o_ref[...] += jnp.dot(a.astype(BF16), wd_ref[...], preferred_element_type=F32)

    if final_norm:
        @pl.when(f == pl.num_programs(1) - 1)
        def _():
            o_ref[...] = _rms_rows(o_ref[...], nw_ref[1:2, :])


def _pack_rows(rows, width):
    rows = [jnp.pad(r.astype(F32), (0, width - r.shape[0])) for r in rows]
    rows += [jnp.zeros((width,), F32)] * (-len(rows) % SUBLANE)
    return jnp.stack(rows)


def _ffn(x, norm_w, wg, wu, wd, layer, final_w, *, tm, tf):
    m, d = x.shape
    d_ff = wg.shape[-1]
    assert m % tm == 0 and d_ff % tf == 0
    final_norm = final_w is not None
    nw = _pack_rows([norm_w[layer]] + ([final_w] if final_norm else []), d)
    return pl.pallas_call(
        functools.partial(_ffn_body, final_norm=final_norm),
        out_shape=jax.ShapeDtypeStruct((m, d), F32),
        grid=(m // tm, d_ff // tf),
        in_specs=[
            pl.BlockSpec((tm, d), lambda i, f: (i, 0)),
            pl.BlockSpec((None, d, tf), lambda i, f: (layer, 0, f)),
            pl.BlockSpec((None, d, tf), lambda i, f: (layer, 0, f)),
            pl.BlockSpec((None, tf, d), lambda i, f: (layer, f, 0)),
            pl.BlockSpec(nw.shape, lambda i, f: (0, 0)),
        ],
        out_specs=pl.BlockSpec((tm, d), lambda i, f: (i, 0)),
        scratch_shapes=[pltpu.VMEM((tm, d), BF16)],
        compiler_params=pltpu.CompilerParams(
            dimension_semantics=("arbitrary", "arbitrary"),
            vmem_limit_bytes=VMEM_LIMIT_BYTES),
        name="ffn",
    )(x, wg, wu, wd, nw)


def _mixer_body(*refs, nseg, seg_len, tiles_per_seq, pos0, has_cache):
    it = iter(refs)
    hc_ref, hp_ref, win_ref, wout_ref, cbw_ref, pw_ref, prm_ref = (next(it) for _ in range(7))
    if has_cache:
        ca_ref, cb_ref, cp_ref = (next(it) for _ in range(3))
    o_ref, na_ref, nb_ref, np_ref = (next(it) for _ in range(4))
    full_a, full_b, full_c, z_even, z_odd, xn_ref, ybuf, dbuf = (next(it) for _ in range(8))

    L = seg_len
    CH = ROW_CHUNK
    NG = CH // SUBLANE
    tm = nseg * L
    n_chunks = tm // CH
    chunks_per_seg = L // CH
    n_cb = win_ref.shape[1] // COL_BLOCK
    cb_per_iter = n_cb // n_chunks
    i = pl.program_id(0)
    je = (i + tiles_per_seq - 1) % tiles_per_seq
    slab = lambda c: slice(c * LANE, (c + 1) * LANE)

    def prm(name, c=None):
        row, lane0, width = MIXER_PARAM_LAYOUT[name]
        lanes = slice(lane0, lane0 + width) if c is None else slice(lane0 + c * LANE, lane0 + (c + 1) * LANE)
        return prm_ref[row:row + 1, lanes]

    conv_a_tap = lambda k, c: prm("conv_a_w%d" % k, c)
    conv_b_tap = lambda k, c: cbw_ref[k:k + 1, slab(c)]
    hist_bufs = ((full_a, HIST_A, D_A), (full_b, HIST_B, D_B), (full_c, HIST_C, D_C))

    @pl.when(i == 0)
    def _():
        z_odd[...] = jnp.zeros(z_odd.shape, F32)

    def init_history():
        cache_refs = (ca_ref, cb_ref, cp_ref) if has_cache else (None,) * 3
        for (buf, hist, width), cache in zip(hist_bufs, cache_refs):
            for s in range(nseg):
                for c in range(width // LANE):
                    buf[s, c, 0:hist, :] = cache[s, :, slab(c)] if has_cache else jnp.zeros((hist, LANE), F32)

    def carry_history():
        for buf, hist, width in hist_bufs:
            for s in range(nseg):
                for c in range(width // LANE):
                    buf[s, c, 0:hist, :] = buf[s, c, L:L + hist, :]

    if tiles_per_seq == 1:
        init_history()
    else:
        first = jnp.logical_or(je == 0, i == 0)
        pl.when(first)(init_history)
        pl.when(jnp.logical_not(first))(carry_history)

    xn_ref[...] = _rms_rows(hc_ref[...], prm("mix_norm")).astype(BF16)

    def run(z_write, z_read):
        def project(cb):
            if isinstance(cb, int):
                cols = slice(cb * COL_BLOCK, (cb + 1) * COL_BLOCK)
            else:
                cols = pl.ds(pl.multiple_of(cb * COL_BLOCK, COL_BLOCK), COL_BLOCK)
            z_write[cb] = jnp.dot(xn_ref[...], win_ref[:, cols], preferred_element_type=F32)

        def z(col, rows):
            return z_read[col // COL_BLOCK, rows, col % COL_BLOCK:col % COL_BLOCK + LANE]

        def tap(buf, s, c, row0):
            return buf[s, c, pl.ds(row0, SUBLANE, stride=1), :]

        def dwconv(buf, weight_tap, n_taps, hist, s, c, t0, newest):
            accs = [None] * NG
            for k in range(n_taps):
                wk = jnp.broadcast_to(weight_tap(k, c), (SUBLANE, LANE))
                for r in range(NG):
                    if k == n_taps - 1:
                        x = newest[r * SUBLANE:(r + 1) * SUBLANE]
                    else:
                        x = tap(buf, s, c, hist - (n_taps - 1) + k + t0 + r * SUBLANE)
                    accs[r] = wk * x if accs[r] is None else accs[r] + wk * x
            return jnp.concatenate(accs, axis=0)

        def mixers(s, t0, rows):
            for c in range(D_A // LANE):
                pa = z(2 * D_A + c * LANE, rows) * z(c * LANE, rows)
                full_a[s, c, pl.ds(HIST_A + t0, CH), :] = pa
                conv = dwconv(full_a, conv_a_tap, K_A, HIST_A, s, c, t0, pa)
                ybuf[rows, slab(c)] = (z(D_A + c * LANE, rows) * conv).astype(BF16)

            b0 = 3 * D_A
            xs = []
            for c in range(D_B // LANE):
                v = z(b0 + c * LANE, rows) * jax.nn.sigmoid(z(b0 + D_B + c * LANE, rows))
                full_b[s, c, pl.ds(HIST_B + t0, CH), :] = v
                xs.append(dwconv(full_b, conv_b_tap, K_B, HIST_B, s, c, t0, v) + prm("conv_b_bias", c))
            mu = jnp.sum(functools.reduce(jnp.add, xs), axis=-1, keepdims=True) * (1.0 / D_B)
            xc = [x - mu for x in xs]
            var = jnp.sum(functools.reduce(jnp.add, [x * x for x in xc]), axis=-1, keepdims=True) * (1.0 / D_B)
            inv = lax.rsqrt(var + EPS)
            for c in range(D_B // LANE):
                y = xc[c] * inv * prm("ln_b_gain", c) + prm("ln_b_bias", c)
                ybuf[rows, D_A + c * LANE:D_A + (c + 1) * LANE] = (y * jax.nn.sigmoid(y)).astype(BF16)

            c0 = 3 * D_A + 2 * D_B
            pos = pos0 + je * L + t0 + lax.broadcasted_iota(jnp.int32, (CH, POOL_GROUP), 0)
            for g, w in enumerate(POOL_WINDOWS):
                u = z(c0 + g * LANE, rows)
                full_c[s, g, pl.ds(HIST_C + t0, CH), :] = u
                accs = [u[r * SUBLANE:(r + 1) * SUBLANE] for r in range(NG)]
                for back in range(1, w):
                    for r in range(NG):
                        accs[r] = accs[r] + tap(full_c, s, g, HIST_C - back + t0 + r * SUBLANE)
                cnt = jnp.minimum(pos + 1, w).astype(F32)
                dbuf[rows, slab(g)] = (jnp.concatenate(accs, axis=0) / cnt - u).astype(BF16)

        def step(it_, carry):
            row0 = pl.multiple_of(it_ * CH, CH)
            if nseg == 1:
                s, t0 = 0, row0
            else:
                s = it_ // chunks_per_seg
                t0 = pl.multiple_of((it_ % chunks_per_seg) * CH, CH)
            mixers(s, t0, pl.ds(row0, CH))
            for b in range(cb_per_iter):
                project(it_ * cb_per_iter + b)
            return carry

        lax.fori_loop(0, n_chunks, step, 0)
        for cb in range(cb_per_iter * n_chunks, n_cb):
            project(cb)

    pl.when(i % 2 == 0)(lambda: run(z_even, z_odd))
    pl.when(i % 2 == 1)(lambda: run(z_odd, z_even))

    y0 = D_A + D_B
    for g in range(len(POOL_WINDOWS)):
        yc = jnp.dot(dbuf[:, slab(g)], pw_ref[g], preferred_element_type=F32) * prm("pool_scale", g)
        ybuf[:, y0 + g * POOL_GROUP:y0 + (g + 1) * POOL_GROUP] = yc.astype(BF16)

    o_ref[...] = hp_ref[...] + jnp.dot(ybuf[...], wout_ref[...], preferred_element_type=F32)

    for (buf, hist, width), out in zip(hist_bufs, (na_ref, nb_ref, np_ref)):
        for s in range(nseg):
            for c in range(width // LANE):
                out[s, :, slab(c)] = buf[s, c, L:L + hist, :]


def _mixer_params(w, layer, d):
    vectors = dict(mix_norm=w["mix_norm"][layer], conv_b_bias=w["conv_b_bias"][layer],
                   ln_b_gain=w["ln_b_gain"][layer], ln_b_bias=w["ln_b_bias"][layer],
                   pool_scale=w["pool_scale"][layer],
                   **{"conv_a_w%d" % k: w["conv_a_w"][layer, k] for k in range(K_A)})
    n_rows = 1 + max(row for row, _, _ in MIXER_PARAM_LAYOUT.values())
    rows = []
    for r in range(n_rows):
        placed = sorted((lane0, name) for name, (row, lane0, _) in MIXER_PARAM_LAYOUT.items() if row == r)
        parts, at = [], 0
        for lane0, name in placed:
            parts += [jnp.zeros((lane0 - at,), F32), vectors[name].astype(F32)]
            at = lane0 + MIXER_PARAM_LAYOUT[name][2]
        rows.append(jnp.concatenate(parts))
    return _pack_rows(rows, d)


def _mixer(h, layer, n_seq, pos0, caches, params, w_in, conv_b_w, pool_w, w_out, *, tm):
    m, d = h.shape
    seq_len = m // n_seq
    if seq_len >= tm:
        nseg, seg_len, tiles_per_seq = 1, tm, seq_len // tm
        assert seq_len % tm == 0
    else:
        nseg, seg_len, tiles_per_seq = tm // seq_len, seq_len, 1
        assert tm % seq_len == 0 and n_seq % nseg == 0
    assert seg_len % ROW_CHUNK == 0 and seg_len >= HIST_B
    has_cache = caches is not None
    d_in = w_in.shape[-1]
    n_tiles = m // tm
    assert d_in % COL_BLOCK == 0 and d_in // COL_BLOCK >= tm // ROW_CHUNK

    cur = lambda i: jnp.minimum(i, n_tiles - 1)
    prev = lambda i: jnp.maximum(i - 1, 0)

    def const(shape):
        nd = len(shape)
        return pl.BlockSpec((None,) + shape, lambda i: (layer,) + (0,) * nd, pipeline_mode=pl.Buffered(1))

    in_specs = [
        pl.BlockSpec((tm, d), lambda i: (cur(i), 0)),
        pl.BlockSpec((tm, d), lambda i: (prev(i), 0)),
        const((d, d_in)),
        const((w_out.shape[1], d)),
        const((K_B, D_B)),
        const((len(POOL_WINDOWS), POOL_GROUP, POOL_GROUP)),
        pl.BlockSpec(params.shape, lambda i: (0, 0), pipeline_mode=pl.Buffered(1)),
    ]
    args = [h, h, w_in, w_out, conv_b_w, pool_w, params]
    if has_cache:
        ca, cb, cp = caches
        for c, hist in ((ca, HIST_A), (cb, HIST_B), (cp, HIST_C)):
            in_specs.append(pl.BlockSpec((None, nseg, hist, c.shape[-1]), lambda i: (layer, prev(i), 0, 0)))
            args.append(c)

    seq_of = lambda i: (prev(i) // tiles_per_seq, 0, 0)
    out_shape = [
        jax.ShapeDtypeStruct((m, d), F32),
        jax.ShapeDtypeStruct((n_seq, HIST_A, D_A), F32),
        jax.ShapeDtypeStruct((n_seq, HIST_B, D_B), F32),
        jax.ShapeDtypeStruct((n_seq, HIST_C, D_C), F32),
    ]
    out_specs = [
        pl.BlockSpec((tm, d), lambda i: (prev(i), 0)),
        pl.BlockSpec((nseg, HIST_A, D_A), seq_of),
        pl.BlockSpec((nseg, HIST_B, D_B), seq_of),
        pl.BlockSpec((nseg, HIST_C, D_C), seq_of),
    ]
    scratch = [
        pltpu.VMEM((nseg, D_A // LANE, HIST_A + seg_len, LANE), F32),
        pltpu.VMEM((nseg, D_B // LANE, HIST_B + seg_len, LANE), F32),
        pltpu.VMEM((nseg, D_C // LANE, HIST_C + seg_len, LANE), F32),
        pltpu.VMEM((d_in // COL_BLOCK, tm, COL_BLOCK), F32),
        pltpu.VMEM((d_in // COL_BLOCK, tm, COL_BLOCK), F32),
        pltpu.VMEM((tm, d), BF16),
        pltpu.VMEM((tm, d), BF16),
        pltpu.VMEM((tm, D_C), BF16),
    ]
    return pl.pallas_call(
        functools.partial(_mixer_body, nseg=nseg, seg_len=seg_len, tiles_per_seq=tiles_per_seq,
                          pos0=pos0, has_cache=has_cache),
        out_shape=out_shape,
        grid=(n_tiles + 1,),
        in_specs=in_specs,
        out_specs=out_specs,
        scratch_shapes=scratch,
        compiler_params=pltpu.CompilerParams(
            dimension_semantics=("arbitrary",),
            vmem_limit_bytes=VMEM_LIMIT_BYTES),
        name="mixer",
    )(*args)


def _front_pad(cache, hist):
    pad = hist - cache.shape[2]
    return jnp.pad(cache, ((0, 0), (0, 0), (pad, 0), (0, 0)))


def _trunk(x, pos0, caches, w, *, tm_ffn, tf, tm_mix):
    n_seq, seq_len, d = x.shape
    h = x.reshape(n_seq * seq_len, d)
    depth = w["w_in"].shape[0]
    new_a, new_b, new_p = [], [], []
    for l in range(depth):
        h = _ffn(h, w["ffn1_norm"], w["ffn1_wg"], w["ffn1_wu"], w["ffn1_wd"], l, None, tm=tm_ffn, tf=tf)
        h, na, nb, npool = _mixer(h, l, n_seq, pos0, caches, _mixer_params(w, l, d), w["w_in"],
                                  w["conv_b_w"], w["pool_w"], w["w_out"], tm=tm_mix)
        final_w = w["final_norm"] if l == depth - 1 else None
        h = _ffn(h, w["ffn2_norm"], w["ffn2_wg"], w["ffn2_wu"], w["ffn2_wd"], l, final_w, tm=tm_ffn, tf=tf)
        new_a.append(na[:, HIST_A - (K_A - 1):])
        new_b.append(nb[:, HIST_B - (K_B - 1):])
        new_p.append(npool[:, HIST_C - POOL_PAD:])
    return h.reshape(n_seq, seq_len, d), jnp.stack(new_a), jnp.stack(new_b), jnp.stack(new_p)


def kernel(x_prompt, x_sample, cache_conv_a, cache_conv_b, cache_pool, ffn1_norm, ffn1_wg, ffn1_wu, ffn1_wd,
           mix_norm, w_in, conv_a_w, conv_b_w, conv_b_bias, ln_b_gain, ln_b_bias, pool_w, pool_scale, w_out,
           ffn2_norm, ffn2_wg, ffn2_wu, ffn2_wd, final_norm):
    w = dict(
        ffn1_norm=ffn1_norm, ffn1_wg=ffn1_wg.astype(BF16), ffn1_wu=ffn1_wu.astype(BF16),
        ffn1_wd=ffn1_wd.astype(BF16), mix_norm=mix_norm, w_in=w_in.astype(BF16), conv_a_w=conv_a_w,
        conv_b_w=conv_b_w, conv_b_bias=conv_b_bias, ln_b_gain=ln_b_gain, ln_b_bias=ln_b_bias,
        pool_w=pool_w.astype(BF16), pool_scale=pool_scale, w_out=w_out.astype(BF16),
        ffn2_norm=ffn2_norm, ffn2_wg=ffn2_wg.astype(BF16), ffn2_wu=ffn2_wu.astype(BF16),
        ffn2_wd=ffn2_wd.astype(BF16), final_norm=final_norm)

    n_dec = x_sample.shape[0] * x_sample.shape[1]
    y_p, a_p, b_p, p_p = _trunk(x_prompt, 0, None, w, tm_ffn=1024, tf=512, tm_mix=256)
    caches = (_front_pad(cache_conv_a, HIST_A), _front_pad(cache_conv_b, HIST_B),
              _front_pad(cache_pool, HIST_C))
    y_s, a_s, b_s, p_s = _trunk(x_sample, PAST_LEN, caches, w, tm_ffn=n_dec, tf=512, tm_mix=256)
    return (y_p, y_s, a_p, b_p, p_p, a_s, b_s, p_s)
```

```python
import functools

import jax
import jax.numpy as jnp
from jax import lax
from jax.experimental import pallas as pl
from jax.experimental.pallas import tpu as pltpu

EPS = 1e-6
PAST_LEN = 1024
D_MODEL = 2048
D_A = 768
D_B = 768
D_C = 512
K_A = 3
K_B = 31
POOL_WINDOWS = (2, 4, 8, 16)
POOL_GROUP = D_C // len(POOL_WINDOWS)
POOL_PAD = max(POOL_WINDOWS) - 1

LANE = 128
SUBLANE = 8
COL_BLOCK = 256
HIST_A = 32
HIST_B = 32
HIST_C = 16
ROW_CHUNK = 32
MIXER_PARAM_LAYOUT = {
    "mix_norm": (0, 0, D_MODEL),
    "conv_a_w0": (1, 0, D_A), "conv_a_w1": (1, D_A, D_A),
    "conv_a_w2": (2, 0, D_A), "conv_b_bias": (2, D_A, D_B),
    "ln_b_gain": (3, 0, D_B), "ln_b_bias": (3, D_B, D_B),
    "pool_scale": (4, 0, D_C),
}

VMEM_LIMIT_BYTES = 60 * 1024 * 1024

F32 = jnp.float32
BF16 = jnp.bfloat16


def _rms_rows(x, w):
    ms = jnp.mean(x * x, axis=-1, keepdims=True)
    return (x * lax.rsqrt(ms + EPS)) * w


def _ffn_body(x_ref, wg_ref, wu_ref, wd_ref, nw_ref, o_ref, xn_ref, *, final_norm):
    f = pl.program_id(1)

    @pl.when(f == 0)
    def _():
        x = x_ref[...]
        xn_ref[...] = _rms_rows(x, nw_ref[0:1, :]).astype(BF16)
        o_ref[...] = x

    xn = xn_ref[...]
    g = jnp.dot(xn, wg_ref[...], preferred_element_type=F32)
    u = jnp.dot(xn, wu_ref[...], preferred_element_type=F32)
    a = ((0.5 * g) * jax.nn.sigmoid(g)) * u
    o_ref[...] += jnp.dot(a.astype(BF16), wd_ref[...].astype(BF16), preferred_element_type=F32)

    if final_norm:
        @pl.when(f == pl.num_programs(1) - 1)
        def _():
            o_ref[...] = _rms_rows(o_ref[...], nw_ref[1:2, :])


def _pack_rows(rows, width):
    rows = [jnp.pad(r.astype(F32), (0, width - r.shape[0])) for r in rows]
    rows += [jnp.zeros((width,), F32)] * (-len(rows) % SUBLANE)
    return jnp.stack(rows)


def _ffn(x, norm_w, wg, wu, wd, layer, final_w, *, tm, tf):
    m, d = x.shape
    d_ff = wg.shape[-1]
    assert m % tm == 0 and d_ff % tf == 0
    final_norm = final_w is not None
    nw = _pack_rows([norm_w[layer]] + ([final_w] if final_norm else []), d)
    return pl.pallas_call(
        functools.partial(_ffn_body, final_norm=final_norm),
        out_shape=jax.ShapeDtypeStruct((m, d), F32),
        grid=(m // tm, d_ff // tf),
        in_specs=[
            pl.BlockSpec((tm, d), lambda i, f: (i, 0)),
            pl.BlockSpec((None, d, tf), lambda i, f: (layer, 0, f)),
            pl.BlockSpec((None, d, tf), lambda i, f: (layer, 0, f)),
            pl.BlockSpec((None, tf, d), lambda i, f: (layer, f, 0)),
            pl.BlockSpec(nw.shape, lambda i, f: (0, 0)),
        ],
        out_specs=pl.BlockSpec((tm, d), lambda i, f: (i, 0)),
        scratch_shapes=[pltpu.VMEM((tm, d), BF16)],
        compiler_params=pltpu.CompilerParams(
            dimension_semantics=("arbitrary", "arbitrary"),
            vmem_limit_bytes=VMEM_LIMIT_BYTES),
        name="ffn",
    )(x, wg, wu, wd, nw)


def _mixer_body(*refs, nseg, seg_len, tiles_per_seq, pos0, has_cache):
    it = iter(refs)
    hc_ref, hp_ref, win_ref, wout_ref, cbw_ref, pw_ref, prm_ref = (next(it) for _ in range(7))
    if has_cache:
        ca_ref, cb_ref, cp_ref = (next(it) for _ in range(3))
    o_ref, na_ref, nb_ref, np_ref = (next(it) for _ in range(4))
    full_a, full_b, full_c, z_even, z_odd, xn_ref, ybuf, dbuf = (next(it) for _ in range(8))

    L = seg_len
    CH = ROW_CHUNK
    NG = CH // SUBLANE
    tm = nseg * L
    n_chunks = tm // CH
    chunks_per_seg = L // CH
    n_cb = win_ref.shape[1] // COL_BLOCK
    cb_per_iter = n_cb // n_chunks
    i = pl.program_id(0)
    je = (i + tiles_per_seq - 1) % tiles_per_seq
    slab = lambda c: slice(c * LANE, (c + 1) * LANE)

    def prm(name, c=None):
        row, lane0, width = MIXER_PARAM_LAYOUT[name]
        lanes = slice(lane0, lane0 + width) if c is None else slice(lane0 + c * LANE, lane0 + (c + 1) * LANE)
        return prm_ref[row:row + 1, lanes]

    conv_a_tap = lambda k, c: prm("conv_a_w%d" % k, c)
    conv_b_tap = lambda k, c: cbw_ref[k:k + 1, slab(c)]
    hist_bufs = ((full_a, HIST_A, D_A), (full_b, HIST_B, D_B), (full_c, HIST_C, D_C))

    @pl.when(i == 0)
    def _():
        z_odd[...] = jnp.zeros(z_odd.shape, F32)

    def init_history():
        cache_refs = (ca_ref, cb_ref, cp_ref) if has_cache else (None,) * 3
        for (buf, hist, width), cache in zip(hist_bufs, cache_refs):
            for s in range(nseg):
                for c in range(width // LANE):
                    buf[s, c, 0:hist, :] = cache[s, :, slab(c)] if has_cache else jnp.zeros((hist, LANE), F32)

    def carry_history():
        for buf, hist, width in hist_bufs:
            for s in range(nseg):
                for c in range(width // LANE):
                    buf[s, c, 0:hist, :] = buf[s, c, L:L + hist, :]

    if tiles_per_seq == 1:
        init_history()
    else:
        first = jnp.logical_or(je == 0, i == 0)
        pl.when(first)(init_history)
        pl.when(jnp.logical_not(first))(carry_history)

    xn_ref[...] = _rms_rows(hc_ref[...], prm("mix_norm")).astype(BF16)

    def run(z_write, z_read):
        def project(cb, n=1):
            if isinstance(cb, int):
                cols = slice(cb * COL_BLOCK, (cb + n) * COL_BLOCK)
            else:
                cols = pl.ds(pl.multiple_of(cb * COL_BLOCK, n * COL_BLOCK), n * COL_BLOCK)
            zc = jnp.dot(xn_ref[...], win_ref[:, cols], preferred_element_type=F32)
            for b in range(n):
                z_write[cb + b] = zc[:, b * COL_BLOCK:(b + 1) * COL_BLOCK]

        def z(col, rows):
            return z_read[col // COL_BLOCK, rows, col % COL_BLOCK:col % COL_BLOCK + LANE]

        def tap(buf, s, c, row0):
            return buf[s, c, pl.ds(row0, SUBLANE, stride=1), :]

        def dwconv(buf, weight_tap, n_taps, hist, s, c, t0, newest):
            wks = [jnp.broadcast_to(weight_tap(k, c), (SUBLANE, LANE)) for k in range(n_taps)]
            accs = [None] * NG
            for j in range(n_taps + (NG - 1) * SUBLANE):
                users = [r for r in range(NG) if 0 <= j - r * SUBLANE < n_taps]
                if not users:
                    continue
                if j >= n_taps - 1 and (j - (n_taps - 1)) % SUBLANE == 0:
                    r_new = (j - (n_taps - 1)) // SUBLANE
                    x = newest[r_new * SUBLANE:(r_new + 1) * SUBLANE]
                else:
                    x = tap(buf, s, c, hist - (n_taps - 1) + j + t0)
                for r in users:
                    term = wks[j - r * SUBLANE] * x
                    accs[r] = term if accs[r] is None else accs[r] + term
            return jnp.concatenate(accs, axis=0)

        def mixers(s, t0, rows):
            for c in range(D_A // LANE):
                pa = z(2 * D_A + c * LANE, rows) * z(c * LANE, rows)
                full_a[s, c, pl.ds(HIST_A + t0, CH), :] = pa
                conv = dwconv(full_a, conv_a_tap, K_A, HIST_A, s, c, t0, pa)
                ybuf[rows, slab(c)] = (z(D_A + c * LANE, rows) * conv).astype(BF16)

            b0 = 3 * D_A
            xs = []
            for c in range(D_B // LANE):
                v = z(b0 + c * LANE, rows) * jax.nn.sigmoid(z(b0 + D_B + c * LANE, rows))
                full_b[s, c, pl.ds(HIST_B + t0, CH), :] = v
                xs.append(dwconv(full_b, conv_b_tap, K_B, HIST_B, s, c, t0, v) + prm("conv_b_bias", c))
            mu = jnp.sum(functools.reduce(jnp.add, xs), axis=-1, keepdims=True) * (1.0 / D_B)
            xc = [x - mu for x in xs]
            var = jnp.sum(functools.reduce(jnp.add, [x * x for x in xc]), axis=-1, keepdims=True) * (1.0 / D_B)
            inv = lax.rsqrt(var + EPS)
            for c in range(D_B // LANE):
                y = xc[c] * inv * prm("ln_b_gain", c) + prm("ln_b_bias", c)
                ybuf[rows, D_A + c * LANE:D_A + (c + 1) * LANE] = (y * jax.nn.sigmoid(y)).astype(BF16)

            c0 = 3 * D_A + 2 * D_B
            pos = pos0 + je * L + t0 + lax.broadcasted_iota(jnp.int32, (CH, POOL_GROUP), 0)
            for g, w in enumerate(POOL_WINDOWS):
                u = z(c0 + g * LANE, rows)
                full_c[s, g, pl.ds(HIST_C + t0, CH), :] = u
                accs = [u[r * SUBLANE:(r + 1) * SUBLANE] for r in range(NG)]
                for d in range(-(w - 1), (NG - 1) * SUBLANE):
                    users = [r for r in range(NG) if 1 <= r * SUBLANE - d <= w - 1]
                    if not users:
                        continue
                    if d >= 0 and d % SUBLANE == 0:
                        x = u[d:d + SUBLANE]
                    else:
                        x = tap(full_c, s, g, HIST_C + t0 + d)
                    for r in users:
                        accs[r] = accs[r] + x
                cnt = jnp.minimum(pos + 1, w).astype(F32)
                dbuf[rows, slab(g)] = (jnp.concatenate(accs, axis=0) / cnt - u).astype(BF16)

        def step(it_, carry):
            row0 = pl.multiple_of(it_ * CH, CH)
            if nseg == 1:
                s, t0 = 0, row0
            else:
                s = it_ // chunks_per_seg
                t0 = pl.multiple_of((it_ % chunks_per_seg) * CH, CH)
            mixers(s, t0, pl.ds(row0, CH))
            project(it_ * cb_per_iter, cb_per_iter)
            return carry

        lax.fori_loop(0, n_chunks, step, 0)
        for cb in range(cb_per_iter * n_chunks, n_cb):
            project(cb)

    pl.when(i % 2 == 0)(lambda: run(z_even, z_odd))
    pl.when(i % 2 == 1)(lambda: run(z_odd, z_even))

    y0 = D_A + D_B
    for g in range(len(POOL_WINDOWS)):
        yc = jnp.dot(dbuf[:, slab(g)], pw_ref[g], preferred_element_type=F32) * prm("pool_scale", g)
        ybuf[:, y0 + g * POOL_GROUP:y0 + (g + 1) * POOL_GROUP] = yc.astype(BF16)

    o_ref[...] = hp_ref[...] + jnp.dot(ybuf[...], wout_ref[...], preferred_element_type=F32)

    for (buf, hist, width), out in zip(hist_bufs, (na_ref, nb_ref, np_ref)):
        for s in range(nseg):
            for c in range(width // LANE):
                out[s, :, slab(c)] = buf[s, c, L:L + hist, :]


def _mixer_params(w, layer, d):
    vectors = dict(mix_norm=w["mix_norm"][layer], conv_b_bias=w["conv_b_bias"][layer],
                   ln_b_gain=w["ln_b_gain"][layer], ln_b_bias=w["ln_b_bias"][layer],
                   pool_scale=w["pool_scale"][layer],
                   **{"conv_a_w%d" % k: w["conv_a_w"][layer, k] for k in range(K_A)})
    n_rows = 1 + max(row for row, _, _ in MIXER_PARAM_LAYOUT.values())
    rows = []
    for r in range(n_rows):
        placed = sorted((lane0, name) for name, (row, lane0, _) in MIXER_PARAM_LAYOUT.items() if row == r)
        parts, at = [], 0
        for lane0, name in placed:
            parts += [jnp.zeros((lane0 - at,), F32), vectors[name].astype(F32)]
            at = lane0 + MIXER_PARAM_LAYOUT[name][2]
        rows.append(jnp.concatenate(parts))
    return _pack_rows(rows, d)


def _mixer(h, layer, n_seq, pos0, caches, params, w_in, conv_b_w, pool_w, w_out, *, tm):
    m, d = h.shape
    seq_len = m // n_seq
    if seq_len >= tm:
        nseg, seg_len, tiles_per_seq = 1, tm, seq_len // tm
        assert seq_len % tm == 0
    else:
        nseg, seg_len, tiles_per_seq = tm // seq_len, seq_len, 1
        assert tm % seq_len == 0 and n_seq % nseg == 0
    assert seg_len % ROW_CHUNK == 0 and seg_len >= HIST_B
    has_cache = caches is not None
    d_in = w_in.shape[-1]
    n_tiles = m // tm
    assert d_in % COL_BLOCK == 0 and d_in // COL_BLOCK >= tm // ROW_CHUNK

    cur = lambda i: jnp.minimum(i, n_tiles - 1)
    prev = lambda i: jnp.maximum(i - 1, 0)

    def const(shape):
        nd = len(shape)
        return pl.BlockSpec((None,) + shape, lambda i: (layer,) + (0,) * nd, pipeline_mode=pl.Buffered(1))

    in_specs = [
        pl.BlockSpec((tm, d), lambda i: (cur(i), 0)),
        pl.BlockSpec((tm, d), lambda i: (prev(i), 0)),
        const((d, d_in)),
        const((w_out.shape[1], d)),
        const((K_B, D_B)),
        const((len(POOL_WINDOWS), POOL_GROUP, POOL_GROUP)),
        pl.BlockSpec(params.shape, lambda i: (0, 0), pipeline_mode=pl.Buffered(1)),
    ]
    args = [h, h, w_in, w_out, conv_b_w, pool_w, params]
    if has_cache:
        ca, cb, cp = caches
        for c, hist in ((ca, HIST_A), (cb, HIST_B), (cp, HIST_C)):
            in_specs.append(pl.BlockSpec((None, nseg, hist, c.shape[-1]), lambda i: (layer, prev(i), 0, 0)))
            args.append(c)

    seq_of = lambda i: (prev(i) // tiles_per_seq, 0, 0)
    out_shape = [
        jax.ShapeDtypeStruct((m, d), F32),
        jax.ShapeDtypeStruct((n_seq, HIST_A, D_A), F32),
        jax.ShapeDtypeStruct((n_seq, HIST_B, D_B), F32),
        jax.ShapeDtypeStruct((n_seq, HIST_C, D_C), F32),
    ]
    out_specs = [
        pl.BlockSpec((tm, d), lambda i: (prev(i), 0)),
        pl.BlockSpec((nseg, HIST_A, D_A), seq_of),
        pl.BlockSpec((nseg, HIST_B, D_B), seq_of),
        pl.BlockSpec((nseg, HIST_C, D_C), seq_of),
    ]
    scratch = [
        pltpu.VMEM((nseg, D_A // LANE, HIST_A + seg_len, LANE), F32),
        pltpu.VMEM((nseg, D_B // LANE, HIST_B + seg_len, LANE), F32),
        pltpu.VMEM((nseg, D_C // LANE, HIST_C + seg_len, LANE), F32),
        pltpu.VMEM((d_in // COL_BLOCK, tm, COL_BLOCK), F32),
        pltpu.VMEM((d_in // COL_BLOCK, tm, COL_BLOCK), F32),
        pltpu.VMEM((tm, d), BF16),
        pltpu.VMEM((tm, d), BF16),
        pltpu.VMEM((tm, D_C), BF16),
    ]
    return pl.pallas_call(
        functools.partial(_mixer_body, nseg=nseg, seg_len=seg_len, tiles_per_seq=tiles_per_seq,
                          pos0=pos0, has_cache=has_cache),
        out_shape=out_shape,
        grid=(n_tiles + 1,),
        in_specs=in_specs,
        out_specs=out_specs,
        scratch_shapes=scratch,
        compiler_params=pltpu.CompilerParams(
            dimension_semantics=("arbitrary",),
            vmem_limit_bytes=VMEM_LIMIT_BYTES),
        name="mixer",
    )(*args)


def _front_pad(cache, hist):
    pad = hist - cache.shape[2]
    return jnp.pad(cache, ((0, 0), (0, 0), (pad, 0), (0, 0)))


def _trunk(x, pos0, caches, w, *, tm_ffn, tf, tm_mix):
    n_seq, seq_len, d = x.shape
    h = x.reshape(n_seq * seq_len, d)
    depth = w["w_in"].shape[0]
    new_a, new_b, new_p = [], [], []
    for l in range(depth):
        h = _ffn(h, w["ffn1_norm"], w["ffn1_wg"], w["ffn1_wu"], w["ffn1_wd"], l, None, tm=tm_ffn, tf=tf)
        h, na, nb, npool = _mixer(h, l, n_seq, pos0, caches, _mixer_params(w, l, d), w["w_in"],
                                  w["conv_b_w"], w["pool_w"], w["w_out"], tm=tm_mix)
        final_w = w["final_norm"] if l == depth - 1 else None
        h = _ffn(h, w["ffn2_norm"], w["ffn2_wg"], w["ffn2_wu"], w["ffn2_wd"], l, final_w, tm=tm_ffn, tf=tf)
        new_a.append(na[:, HIST_A - (K_A - 1):])
        new_b.append(nb[:, HIST_B - (K_B - 1):])
        new_p.append(npool[:, HIST_C - POOL_PAD:])
    return h.reshape(n_seq, seq_len, d), jnp.stack(new_a), jnp.stack(new_b), jnp.stack(new_p)


def kernel(x_prompt, x_sample, cache_conv_a, cache_conv_b, cache_pool, ffn1_norm, ffn1_wg, ffn1_wu, ffn1_wd,
           mix_norm, w_in, conv_a_w, conv_b_w, conv_b_bias, ln_b_gain, ln_b_bias, pool_w, pool_scale, w_out,
           ffn2_norm, ffn2_wg, ffn2_wu, ffn2_wd, final_norm):
    w = dict(
        ffn1_norm=ffn1_norm, ffn1_wg=ffn1_wg.astype(BF16), ffn1_wu=ffn1_wu.astype(BF16),
        ffn1_wd=ffn1_wd, mix_norm=mix_norm, w_in=w_in.astype(BF16), conv_a_w=conv_a_w,
        conv_b_w=conv_b_w, conv_b_bias=conv_b_bias, ln_b_gain=ln_b_gain, ln_b_bias=ln_b_bias,
        pool_w=pool_w.astype(BF16), pool_scale=pool_scale, w_out=w_out.astype(BF16),
        ffn2_norm=ffn2_norm, ffn2_wg=ffn2_wg.astype(BF16), ffn2_wu=ffn2_wu.astype(BF16),
        ffn2_wd=ffn2_wd, final_norm=final_norm)

    n_dec = x_sample.shape[0] * x_sample.shape[1]
    y_p, a_p, b_p, p_p = _trunk(x_prompt, 0, None, w, tm_ffn=1024, tf=512, tm_mix=256)
    caches = (_front_pad(cache_conv_a, HIST_A), _front_pad(cache_conv_b, HIST_B),
              _front_pad(cache_pool, HIST_C))
    y_s, a_s, b_s, p_s = _trunk(x_sample, PAST_LEN, caches, w, tm_ffn=n_dec, tf=512, tm_mix=256)
    return (y_p, y_s, a_p, b_p, p_p, a_s, b_s, p_s)
```

```python
import functools

import jax
import jax.numpy as jnp
from jax import lax
from jax.experimental import pallas as pl
from jax.experimental.pallas import tpu as pltpu

EPS = 1e-6
PAST_LEN = 1024
D_MODEL = 2048
D_A = 768
D_B = 768
D_C = 512
K_A = 3
K_B = 31
POOL_WINDOWS = (2, 4, 8, 16)
POOL_GROUP = D_C // len(POOL_WINDOWS)
POOL_PAD = max(POOL_WINDOWS) - 1

LANE = 128
SUBLANE = 8
COL_BLOCK = 256
HIST_A = 32
HIST_B = 32
HIST_C = 16
ROW_CHUNK = 32
MIXER_PARAM_LAYOUT = {
    "mix_norm": (0, 0, D_MODEL),
    "conv_a_w0": (1, 0, D_A), "conv_a_w1": (1, D_A, D_A),
    "conv_a_w2": (2, 0, D_A), "conv_b_bias": (2, D_A, D_B),
    "ln_b_gain": (3, 0, D_B), "ln_b_bias": (3, D_B, D_B),
    "pool_scale": (4, 0, D_C),
}

VMEM_LIMIT_BYTES = 60 * 1024 * 1024

F32 = jnp.float32
BF16 = jnp.bfloat16


def _rms_rows(x, w):
    ms = jnp.mean(x * x, axis=-1, keepdims=True)
    return (x * lax.rsqrt(ms + EPS)) * w


def _ffn_body(x_ref, wg_ref, wu_ref, wd_ref, nw_ref, o_ref, xn_ref, *, final_norm):
    f = pl.program_id(1)

    @pl.when(f == 0)
    def _():
        x = x_ref[...]
        xn_ref[...] = _rms_rows(x, nw_ref[0:1, :]).astype(BF16)
        o_ref[...] = x

    xn = xn_ref[...]
    g = jnp.dot(xn, wg_ref[...], preferred_element_type=F32)
    u = jnp.dot(xn, wu_ref[...], preferred_element_type=F32)
    a = ((0.5 * g) * jax.nn.sigmoid(g)) * u
    o_ref[...] += jnp.dot(a.astype(BF16), wd_ref[...].astype(BF16), preferred_element_type=F32)

    if final_norm:
        @pl.when(f == pl.num_programs(1) - 1)
        def _():
            o_ref[...] = _rms_rows(o_ref[...], nw_ref[1:2, :])


def _pack_rows(rows, width):
    rows = [jnp.pad(r.astype(F32), (0, width - r.shape[0])) for r in rows]
    rows += [jnp.zeros((width,), F32)] * (-len(rows) % SUBLANE)
    return jnp.stack(rows)


def _ffn(x, norm_w, wg, wu, wd, layer, final_w, *, tm, tf):
    m, d = x.shape
    d_ff = wg.shape[-1]
    assert m % tm == 0 and d_ff % tf == 0
    final_norm = final_w is not None
    nw = _pack_rows([norm_w[layer]] + ([final_w] if final_norm else []), d)
    return pl.pallas_call(
        functools.partial(_ffn_body, final_norm=final_norm),
        out_shape=jax.ShapeDtypeStruct((m, d), F32),
        grid=(m // tm, d_ff // tf),
        in_specs=[
            pl.BlockSpec((tm, d), lambda i, f: (i, 0)),
            pl.BlockSpec((None, d, tf), lambda i, f: (layer, 0, f)),
            pl.BlockSpec((None, d, tf), lambda i, f: (layer, 0, f)),
            pl.BlockSpec((None, tf, d), lambda i, f: (layer, f, 0)),
            pl.BlockSpec(nw.shape, lambda i, f: (0, 0)),
        ],
        out_specs=pl.BlockSpec((tm, d), lambda i, f: (i, 0)),
        scratch_shapes=[pltpu.VMEM((tm, d), BF16)],
        compiler_params=pltpu.CompilerParams(
            dimension_semantics=("arbitrary", "arbitrary"),
            vmem_limit_bytes=VMEM_LIMIT_BYTES),
        name="ffn",
    )(x, wg, wu, wd, nw)


def _mixer_body(*refs, nseg, seg_len, tiles_per_seq, pos0, has_cache):
    it = iter(refs)
    hn_ref, hp_ref, win_ref, wout_ref, cbw_ref, pw_ref, prm_ref = (next(it) for _ in range(7))
    if has_cache:
        ca_ref, cb_ref, cp_ref = (next(it) for _ in range(3))
    o_ref, na_ref, nb_ref, np_ref = (next(it) for _ in range(4))
    full_a, full_b, full_c, z_even, z_odd, xn_ref, ybuf, dbuf = (next(it) for _ in range(8))

    L = seg_len
    CH = ROW_CHUNK
    NG = CH // SUBLANE
    tm = nseg * L
    n_chunks = tm // CH
    chunks_per_seg = L // CH
    n_cb = win_ref.shape[1] // COL_BLOCK
    cb_per_iter = n_cb // n_chunks
    i = pl.program_id(0)
    je = (i + tiles_per_seq - 1) % tiles_per_seq
    slab = lambda c: slice(c * LANE, (c + 1) * LANE)

    def prm(name, c=None):
        row, lane0, width = MIXER_PARAM_LAYOUT[name]
        lanes = slice(lane0, lane0 + width) if c is None else slice(lane0 + c * LANE, lane0 + (c + 1) * LANE)
        return prm_ref[row:row + 1, lanes]

    conv_a_tap = lambda k, c: prm("conv_a_w%d" % k, c)
    conv_b_tap = lambda k, c: cbw_ref[k:k + 1, slab(c)]
    hist_bufs = ((full_a, HIST_A, D_A), (full_b, HIST_B, D_B), (full_c, HIST_C, D_C))

    def normalize(h_ref):
        xn_ref[...] = _rms_rows(h_ref[...], prm("mix_norm")).astype(BF16)

    @pl.when(i == 0)
    def _():
        z_odd[...] = jnp.zeros(z_odd.shape, F32)
        normalize(hp_ref)

    def init_history():
        cache_refs = (ca_ref, cb_ref, cp_ref) if has_cache else (None,) * 3
        for (buf, hist, width), cache in zip(hist_bufs, cache_refs):
            for s in range(nseg):
                for c in range(width // LANE):
                    buf[s, c, 0:hist, :] = cache[s, :, slab(c)] if has_cache else jnp.zeros((hist, LANE), F32)

    def carry_history():
        for buf, hist, width in hist_bufs:
            for s in range(nseg):
                for c in range(width // LANE):
                    buf[s, c, 0:hist, :] = buf[s, c, L:L + hist, :]

    if tiles_per_seq == 1:
        init_history()
    else:
        first = jnp.logical_or(je == 0, i == 0)
        pl.when(first)(init_history)
        pl.when(jnp.logical_not(first))(carry_history)

    def run(z_write, z_read):
        def project(cb, n=1):
            if isinstance(cb, int):
                cols = slice(cb * COL_BLOCK, (cb + n) * COL_BLOCK)
            else:
                cols = pl.ds(pl.multiple_of(cb * COL_BLOCK, n * COL_BLOCK), n * COL_BLOCK)
            zc = jnp.dot(xn_ref[...], win_ref[:, cols], preferred_element_type=F32)
            for b in range(n):
                z_write[cb + b] = zc[:, b * COL_BLOCK:(b + 1) * COL_BLOCK]

        def z(col, rows):
            return z_read[col // COL_BLOCK, rows, col % COL_BLOCK:col % COL_BLOCK + LANE]

        def tap(buf, s, c, row0):
            return buf[s, c, pl.ds(row0, SUBLANE, stride=1), :]

        def dwconv(buf, weight_tap, n_taps, hist, s, c, t0, newest):
            accs = [None] * NG
            for k in range(n_taps):
                wk = jnp.broadcast_to(weight_tap(k, c), (SUBLANE, LANE))
                for r in range(NG):
                    if k == n_taps - 1:
                        x = newest[r * SUBLANE:(r + 1) * SUBLANE]
                    else:
                        x = tap(buf, s, c, hist - (n_taps - 1) + k + t0 + r * SUBLANE)
                    accs[r] = wk * x if accs[r] is None else accs[r] + wk * x
            return jnp.concatenate(accs, axis=0)

        def mixers(s, t0, rows):
            for c in range(D_A // LANE):
                pa = z(2 * D_A + c * LANE, rows) * z(c * LANE, rows)
                full_a[s, c, pl.ds(HIST_A + t0, CH), :] = pa
                conv = dwconv(full_a, conv_a_tap, K_A, HIST_A, s, c, t0, pa)
                ybuf[rows, slab(c)] = (z(D_A + c * LANE, rows) * conv).astype(BF16)

            b0 = 3 * D_A
            xs = []
            for c in range(D_B // LANE):
                v = z(b0 + c * LANE, rows) * jax.nn.sigmoid(z(b0 + D_B + c * LANE, rows))
                full_b[s, c, pl.ds(HIST_B + t0, CH), :] = v
                xs.append(dwconv(full_b, conv_b_tap, K_B, HIST_B, s, c, t0, v) + prm("conv_b_bias", c))
            mu = jnp.sum(functools.reduce(jnp.add, xs), axis=-1, keepdims=True) * (1.0 / D_B)
            xc = [x - mu for x in xs]
            var = jnp.sum(functools.reduce(jnp.add, [x * x for x in xc]), axis=-1, keepdims=True) * (1.0 / D_B)
            inv = lax.rsqrt(var + EPS)
            for c in range(D_B // LANE):
                y = xc[c] * inv * prm("ln_b_gain", c) + prm("ln_b_bias", c)
                ybuf[rows, D_A + c * LANE:D_A + (c + 1) * LANE] = (y * jax.nn.sigmoid(y)).astype(BF16)

            c0 = 3 * D_A + 2 * D_B
            pos = pos0 + je * L + t0 + lax.broadcasted_iota(jnp.int32, (CH, POOL_GROUP), 0)
            for g, w in enumerate(POOL_WINDOWS):
                u = z(c0 + g * LANE, rows)
                full_c[s, g, pl.ds(HIST_C + t0, CH), :] = u
                accs = [u[r * SUBLANE:(r + 1) * SUBLANE] for r in range(NG)]
                for d in range(-(w - 1), (NG - 1) * SUBLANE):
                    users = [r for r in range(NG) if 1 <= r * SUBLANE - d <= w - 1]
                    if not users:
                        continue
                    if d >= 0 and d % SUBLANE == 0:
                        x = u[d:d + SUBLANE]
                    else:
                        x = tap(full_c, s, g, HIST_C + t0 + d)
                    for r in users:
                        accs[r] = accs[r] + x
                cnt = jnp.minimum(pos + 1, w).astype(F32)
                dbuf[rows, slab(g)] = (jnp.concatenate(accs, axis=0) / cnt - u).astype(BF16)

        def step(it_, carry):
            row0 = pl.multiple_of(it_ * CH, CH)
            if nseg == 1:
                s, t0 = 0, row0
            else:
                s = it_ // chunks_per_seg
                t0 = pl.multiple_of((it_ % chunks_per_seg) * CH, CH)
            mixers(s, t0, pl.ds(row0, CH))
            project(it_ * cb_per_iter, cb_per_iter)
            return carry

        lax.fori_loop(0, n_chunks, step, 0)
        for cb in range(cb_per_iter * n_chunks, n_cb):
            project(cb)

    pl.when(i % 2 == 0)(lambda: run(z_even, z_odd))
    pl.when(i % 2 == 1)(lambda: run(z_odd, z_even))

    y0 = D_A + D_B
    for g in range(len(POOL_WINDOWS)):
        yc = jnp.dot(dbuf[:, slab(g)], pw_ref[g], preferred_element_type=F32) * prm("pool_scale", g)
        ybuf[:, y0 + g * POOL_GROUP:y0 + (g + 1) * POOL_GROUP] = yc.astype(BF16)

    o_ref[...] = hp_ref[...] + jnp.dot(ybuf[...], wout_ref[...], preferred_element_type=F32)
    normalize(hn_ref)

    for (buf, hist, width), out in zip(hist_bufs, (na_ref, nb_ref, np_ref)):
        for s in range(nseg):
            for c in range(width // LANE):
                out[s, :, slab(c)] = buf[s, c, L:L + hist, :]


def _mixer_params(w, layer, d):
    vectors = dict(mix_norm=w["mix_norm"][layer], conv_b_bias=w["conv_b_bias"][layer],
                   ln_b_gain=w["ln_b_gain"][layer], ln_b_bias=w["ln_b_bias"][layer],
                   pool_scale=w["pool_scale"][layer],
                   **{"conv_a_w%d" % k: w["conv_a_w"][layer, k] for k in range(K_A)})
    n_rows = 1 + max(row for row, _, _ in MIXER_PARAM_LAYOUT.values())
    rows = []
    for r in range(n_rows):
        placed = sorted((lane0, name) for name, (row, lane0, _) in MIXER_PARAM_LAYOUT.items() if row == r)
        parts, at = [], 0
        for lane0, name in placed:
            parts += [jnp.zeros((lane0 - at,), F32), vectors[name].astype(F32)]
            at = lane0 + MIXER_PARAM_LAYOUT[name][2]
        rows.append(jnp.concatenate(parts))
    return _pack_rows(rows, d)


def _mixer(h, layer, n_seq, pos0, caches, params, w_in, conv_b_w, pool_w, w_out, *, tm):
    m, d = h.shape
    seq_len = m // n_seq
    if seq_len >= tm:
        nseg, seg_len, tiles_per_seq = 1, tm, seq_len // tm
        assert seq_len % tm == 0
    else:
        nseg, seg_len, tiles_per_seq = tm // seq_len, seq_len, 1
        assert tm % seq_len == 0 and n_seq % nseg == 0
    assert seg_len % ROW_CHUNK == 0 and seg_len >= HIST_B
    has_cache = caches is not None
    d_in = w_in.shape[-1]
    n_tiles = m // tm
    assert d_in % COL_BLOCK == 0 and d_in // COL_BLOCK >= tm // ROW_CHUNK

    nxt = lambda i: jnp.minimum(i + 1, n_tiles - 1)
    prev = lambda i: jnp.maximum(i - 1, 0)

    def const(shape):
        nd = len(shape)
        return pl.BlockSpec((None,) + shape, lambda i: (layer,) + (0,) * nd, pipeline_mode=pl.Buffered(1))

    in_specs = [
        pl.BlockSpec((tm, d), lambda i: (nxt(i), 0)),
        pl.BlockSpec((tm, d), lambda i: (prev(i), 0)),
        const((d, d_in)),
        const((w_out.shape[1], d)),
        const((K_B, D_B)),
        const((len(POOL_WINDOWS), POOL_GROUP, POOL_GROUP)),
        pl.BlockSpec(params.shape, lambda i: (0, 0), pipeline_mode=pl.Buffered(1)),
    ]
    args = [h, h, w_in, w_out, conv_b_w, pool_w, params]
    if has_cache:
        ca, cb, cp = caches
        for c, hist in ((ca, HIST_A), (cb, HIST_B), (cp, HIST_C)):
            in_specs.append(pl.BlockSpec((None, nseg, hist, c.shape[-1]), lambda i: (layer, prev(i), 0, 0)))
            args.append(c)

    seq_of = lambda i: (prev(i) // tiles_per_seq, 0, 0)
    out_shape = [
        jax.ShapeDtypeStruct((m, d), F32),
        jax.ShapeDtypeStruct((n_seq, HIST_A, D_A), F32),
        jax.ShapeDtypeStruct((n_seq, HIST_B, D_B), F32),
        jax.ShapeDtypeStruct((n_seq, HIST_C, D_C), F32),
    ]
    out_specs = [
        pl.BlockSpec((tm, d), lambda i: (prev(i), 0)),
        pl.BlockSpec((nseg, HIST_A, D_A), seq_of),
        pl.BlockSpec((nseg, HIST_B, D_B), seq_of),
        pl.BlockSpec((nseg, HIST_C, D_C), seq_of),
    ]
    scratch = [
        pltpu.VMEM((nseg, D_A // LANE, HIST_A + seg_len, LANE), F32),
        pltpu.VMEM((nseg, D_B // LANE, HIST_B + seg_len, LANE), F32),
        pltpu.VMEM((nseg, D_C // LANE, HIST_C + seg_len, LANE), F32),
        pltpu.VMEM((d_in // COL_BLOCK, tm, COL_BLOCK), F32),
        pltpu.VMEM((d_in // COL_BLOCK, tm, COL_BLOCK), F32),
        pltpu.VMEM((tm, d), BF16),
        pltpu.VMEM((tm, d), BF16),
        pltpu.VMEM((tm, D_C), BF16),
    ]
    return pl.pallas_call(
        functools.partial(_mixer_body, nseg=nseg, seg_len=seg_len, tiles_per_seq=tiles_per_seq,
                          pos0=pos0, has_cache=has_cache),
        out_shape=out_shape,
        grid=(n_tiles + 1,),
        in_specs=in_specs,
        out_specs=out_specs,
        scratch_shapes=scratch,
        compiler_params=pltpu.CompilerParams(
            dimension_semantics=("arbitrary",),
            vmem_limit_bytes=VMEM_LIMIT_BYTES),
        name="mixer",
    )(*args)


def _front_pad(cache, hist):
    pad = hist - cache.shape[2]
    return jnp.pad(cache, ((0, 0), (0, 0), (pad, 0), (0, 0)))


def _trunk(x, pos0, caches, w, *, tm_ffn, tf, tm_mix):
    n_seq, seq_len, d = x.shape
    h = x.reshape(n_seq * seq_len, d)
    depth = w["w_in"].shape[0]
    new_a, new_b, new_p = [], [], []
    for l in range(depth):
        h = _ffn(h, w["ffn1_norm"], w["ffn1_wg"], w["ffn1_wu"], w["ffn1_wd"], l, None, tm=tm_ffn, tf=tf)
        h, na, nb, npool = _mixer(h, l, n_seq, pos0, caches, _mixer_params(w, l, d), w["w_in"],
                                  w["conv_b_w"], w["pool_w"], w["w_out"], tm=tm_mix)
        final_w = w["final_norm"] if l == depth - 1 else None
        h = _ffn(h, w["ffn2_norm"], w["ffn2_wg"], w["ffn2_wu"], w["ffn2_wd"], l, final_w, tm=tm_ffn, tf=tf)
        new_a.append(na[:, HIST_A - (K_A - 1):])
        new_b.append(nb[:, HIST_B - (K_B - 1):])
        new_p.append(npool[:, HIST_C - POOL_PAD:])
    return h.reshape(n_seq, seq_len, d), jnp.stack(new_a), jnp.stack(new_b), jnp.stack(new_p)


def kernel(x_prompt, x_sample, cache_conv_a, cache_conv_b, cache_pool, ffn1_norm, ffn1_wg, ffn1_wu, ffn1_wd,
           mix_norm, w_in, conv_a_w, conv_b_w, conv_b_bias, ln_b_gain, ln_b_bias, pool_w, pool_scale, w_out,
           ffn2_norm, ffn2_wg, ffn2_wu, ffn2_wd, final_norm):
    w = dict(
        ffn1_norm=ffn1_norm, ffn1_wg=ffn1_wg.astype(BF16), ffn1_wu=ffn1_wu.astype(BF16),
        ffn1_wd=ffn1_wd, mix_norm=mix_norm, w_in=w_in.astype(BF16), conv_a_w=conv_a_w,
        conv_b_w=conv_b_w, conv_b_bias=conv_b_bias, ln_b_gain=ln_b_gain, ln_b_bias=ln_b_bias,
        pool_w=pool_w.astype(BF16), pool_scale=pool_scale, w_out=w_out.astype(BF16),
        ffn2_norm=ffn2_norm, ffn2_wg=ffn2_wg.astype(BF16), ffn2_wu=ffn2_wu.astype(BF16),
        ffn2_wd=ffn2_wd, final_norm=final_norm)

    n_dec = x_sample.shape[0] * x_sample.shape[1]
    y_p, a_p, b_p, p_p = _trunk(x_prompt, 0, None, w, tm_ffn=1024, tf=512, tm_mix=256)
    caches = (_front_pad(cache_conv_a, HIST_A), _front_pad(cache_conv_b, HIST_B),
              _front_pad(cache_pool, HIST_C))
    y_s, a_s, b_s, p_s = _trunk(x_sample, PAST_LEN, caches, w, tm_ffn=n_dec, tf=512, tm_mix=256)
    return (y_p, y_s, a_p, b_p, p_p, a_s, b_s, p_s)
```

```python
import functools

import jax
import jax.numpy as jnp
from jax import lax
from jax.experimental import pallas as pl
from jax.experimental.pallas import tpu as pltpu

EPS = 1e-6
PAST_LEN = 1024
D_MODEL = 2048
D_A = 768
D_B = 768
D_C = 512
K_A = 3
K_B = 31
POOL_WINDOWS = (2, 4, 8, 16)
POOL_GROUP = D_C // len(POOL_WINDOWS)
POOL_PAD = max(POOL_WINDOWS) - 1

LANE = 128
SUBLANE = 8
COL_BLOCK = 256
HIST_A = 32
HIST_B = 32
HIST_C = 16
ROW_CHUNK = 32
MIXER_PARAM_LAYOUT = {
    "mix_norm": (0, 0, D_MODEL),
    "conv_a_w0": (1, 0, D_A), "conv_a_w1": (1, D_A, D_A),
    "conv_a_w2": (2, 0, D_A), "conv_b_bias": (2, D_A, D_B),
    "ln_b_gain": (3, 0, D_B), "ln_b_bias": (3, D_B, D_B),
    "pool_scale": (4, 0, D_C),
}

VMEM_LIMIT_BYTES = 60 * 1024 * 1024

F32 = jnp.float32
BF16 = jnp.bfloat16


def _rms_rows(x, w):
    ms = jnp.mean(x * x, axis=-1, keepdims=True)
    return (x * lax.rsqrt(ms + EPS)) * w


def _ffn_body(x_ref, wg_ref, wu_ref, wd_ref, nw_ref, o_ref, xn_ref, *, final_norm):
    f = pl.program_id(1)

    @pl.when(f == 0)
    def _():
        x = x_ref[...]
        xn_ref[...] = _rms_rows(x, nw_ref[0:1, :]).astype(BF16)
        o_ref[...] = x

    xn = xn_ref[...]
    g = jnp.dot(xn, wg_ref[...].astype(BF16), preferred_element_type=F32)
    u = jnp.dot(xn, wu_ref[...].astype(BF16), preferred_element_type=F32)
    a = ((0.5 * g) * jax.nn.sigmoid(g)) * u
    o_ref[...] += jnp.dot(a.astype(BF16), wd_ref[...].astype(BF16), preferred_element_type=F32)

    if final_norm:
        @pl.when(f == pl.num_programs(1) - 1)
        def _():
            o_ref[...] = _rms_rows(o_ref[...], nw_ref[1:2, :])


def _pack_rows(rows, width):
    rows = [jnp.pad(r.astype(F32), (0, width - r.shape[0])) for r in rows]
    rows += [jnp.zeros((width,), F32)] * (-len(rows) % SUBLANE)
    return jnp.stack(rows)


def _ffn(x, norm_w, wg, wu, wd, layer, final_w, *, tm, tf):
    m, d = x.shape
    d_ff = wg.shape[-1]
    assert m % tm == 0 and d_ff % tf == 0
    final_norm = final_w is not None
    nw = _pack_rows([norm_w[layer]] + ([final_w] if final_norm else []), d)
    return pl.pallas_call(
        functools.partial(_ffn_body, final_norm=final_norm),
        out_shape=jax.ShapeDtypeStruct((m, d), F32),
        grid=(m // tm, d_ff // tf),
        in_specs=[
            pl.BlockSpec((tm, d), lambda i, f: (i, 0)),
            pl.BlockSpec((None, d, tf), lambda i, f: (layer, 0, f)),
            pl.BlockSpec((None, d, tf), lambda i, f: (layer, 0, f)),
            pl.BlockSpec((None, tf, d), lambda i, f: (layer, f, 0)),
            pl.BlockSpec(nw.shape, lambda i, f: (0, 0)),
        ],
        out_specs=pl.BlockSpec((tm, d), lambda i, f: (i, 0)),
        scratch_shapes=[pltpu.VMEM((tm, d), BF16)],
        compiler_params=pltpu.CompilerParams(
            dimension_semantics=("arbitrary", "arbitrary"),
            vmem_limit_bytes=VMEM_LIMIT_BYTES),
        name="ffn",
    )(x, wg, wu, wd, nw)


def _mixer_body(*refs, nseg, seg_len, tiles_per_seq, pos0, has_cache):
    it = iter(refs)
    hn_ref, hp_ref, win_ref, wout_ref, cbw_ref, pw_ref, prm_ref = (next(it) for _ in range(7))
    if has_cache:
        ca_ref, cb_ref, cp_ref = (next(it) for _ in range(3))
    o_ref, na_ref, nb_ref, np_ref = (next(it) for _ in range(4))
    full_a, full_b, full_c, z_even, z_odd, xn_ref, ybuf, dbuf = (next(it) for _ in range(8))

    L = seg_len
    CH = ROW_CHUNK
    NG = CH // SUBLANE
    tm = nseg * L
    n_chunks = tm // CH
    chunks_per_seg = L // CH
    n_cb = win_ref.shape[1] // COL_BLOCK
    cb_per_iter = n_cb // n_chunks
    i = pl.program_id(0)
    je = (i + tiles_per_seq - 1) % tiles_per_seq
    slab = lambda c: slice(c * LANE, (c + 1) * LANE)

    def prm(name, c=None):
        row, lane0, width = MIXER_PARAM_LAYOUT[name]
        lanes = slice(lane0, lane0 + width) if c is None else slice(lane0 + c * LANE, lane0 + (c + 1) * LANE)
        return prm_ref[row:row + 1, lanes]

    conv_a_tap = lambda k, c: prm("conv_a_w%d" % k, c)
    conv_b_tap = lambda k, c: cbw_ref[k:k + 1, slab(c)]
    hist_bufs = ((full_a, HIST_A, D_A), (full_b, HIST_B, D_B), (full_c, HIST_C, D_C))

    def normalize(h_ref):
        xn_ref[...] = _rms_rows(h_ref[...], prm("mix_norm")).astype(BF16)

    @pl.when(i == 0)
    def _():
        z_odd[...] = jnp.zeros(z_odd.shape, F32)
        normalize(hp_ref)

    def init_history():
        cache_refs = (ca_ref, cb_ref, cp_ref) if has_cache else (None,) * 3
        for (buf, hist, width), cache in zip(hist_bufs, cache_refs):
            for s in range(nseg):
                for c in range(width // LANE):
                    buf[s, c, 0:hist, :] = cache[s, :, slab(c)] if has_cache else jnp.zeros((hist, LANE), F32)

    def carry_history():
        for buf, hist, width in hist_bufs:
            for s in range(nseg):
                for c in range(width // LANE):
                    buf[s, c, 0:hist, :] = buf[s, c, L:L + hist, :]

    if tiles_per_seq == 1:
        init_history()
    else:
        first = jnp.logical_or(je == 0, i == 0)
        pl.when(first)(init_history)
        pl.when(jnp.logical_not(first))(carry_history)

    def run(z_write, z_read):
        def project(cb, n=1):
            if isinstance(cb, int):
                cols = slice(cb * COL_BLOCK, (cb + n) * COL_BLOCK)
            else:
                cols = pl.ds(pl.multiple_of(cb * COL_BLOCK, n * COL_BLOCK), n * COL_BLOCK)
            zc = jnp.dot(xn_ref[...], win_ref[:, cols], preferred_element_type=F32)
            for b in range(n):
                z_write[cb + b] = zc[:, b * COL_BLOCK:(b + 1) * COL_BLOCK]

        def z(col, rows):
            return z_read[col // COL_BLOCK, rows, col % COL_BLOCK:col % COL_BLOCK + LANE]

        def tap(buf, s, c, row0):
            return buf[s, c, pl.ds(row0, SUBLANE, stride=1), :]

        def dwconv(buf, weight_tap, n_taps, hist, s, c, t0, newest):
            accs = [None] * NG
            for k in range(n_taps):
                wk = jnp.broadcast_to(weight_tap(k, c), (SUBLANE, LANE))
                for r in range(NG):
                    if k == n_taps - 1:
                        x = newest[r * SUBLANE:(r + 1) * SUBLANE]
                    else:
                        x = tap(buf, s, c, hist - (n_taps - 1) + k + t0 + r * SUBLANE)
                    accs[r] = wk * x if accs[r] is None else accs[r] + wk * x
            return jnp.concatenate(accs, axis=0)

        def mixers(s, t0, rows):
            for c in range(D_A // LANE):
                pa = z(2 * D_A + c * LANE, rows) * z(c * LANE, rows)
                full_a[s, c, pl.ds(HIST_A + t0, CH), :] = pa
                conv = dwconv(full_a, conv_a_tap, K_A, HIST_A, s, c, t0, pa)
                ybuf[rows, slab(c)] = (z(D_A + c * LANE, rows) * conv).astype(BF16)

            b0 = 3 * D_A
            xs = []
            for c in range(D_B // LANE):
                v = z(b0 + c * LANE, rows) * jax.nn.sigmoid(z(b0 + D_B + c * LANE, rows))
                full_b[s, c, pl.ds(HIST_B + t0, CH), :] = v
                xs.append(dwconv(full_b, conv_b_tap, K_B, HIST_B, s, c, t0, v) + prm("conv_b_bias", c))
            mu = jnp.sum(functools.reduce(jnp.add, xs), axis=-1, keepdims=True) * (1.0 / D_B)
            xc = [x - mu for x in xs]
            var = jnp.sum(functools.reduce(jnp.add, [x * x for x in xc]), axis=-1, keepdims=True) * (1.0 / D_B)
            inv = lax.rsqrt(var + EPS)
            for c in range(D_B // LANE):
                y = xc[c] * inv * prm("ln_b_gain", c) + prm("ln_b_bias", c)
                ybuf[rows, D_A + c * LANE:D_A + (c + 1) * LANE] = (y * jax.nn.sigmoid(y)).astype(BF16)

            c0 = 3 * D_A + 2 * D_B
            pos = pos0 + je * L + t0 + lax.broadcasted_iota(jnp.int32, (CH, POOL_GROUP), 0)
            for g, w in enumerate(POOL_WINDOWS):
                u = z(c0 + g * LANE, rows)
                full_c[s, g, pl.ds(HIST_C + t0, CH), :] = u
                accs = [u[r * SUBLANE:(r + 1) * SUBLANE] for r in range(NG)]
                for d in range(-(w - 1), (NG - 1) * SUBLANE):
                    users = [r for r in range(NG) if 1 <= r * SUBLANE - d <= w - 1]
                    if not users:
                        continue
                    if d >= 0 and d % SUBLANE == 0:
                        x = u[d:d + SUBLANE]
                    else:
                        x = tap(full_c, s, g, HIST_C + t0 + d)
                    for r in users:
                        accs[r] = accs[r] + x
                cnt = jnp.minimum(pos + 1, w).astype(F32)
                dbuf[rows, slab(g)] = (jnp.concatenate(accs, axis=0) / cnt - u).astype(BF16)

        def step(it_, carry):
            row0 = pl.multiple_of(it_ * CH, CH)
            if nseg == 1:
                s, t0 = 0, row0
            else:
                s = it_ // chunks_per_seg
                t0 = pl.multiple_of((it_ % chunks_per_seg) * CH, CH)
            mixers(s, t0, pl.ds(row0, CH))
            project(it_ * cb_per_iter, cb_per_iter)
            return carry

        lax.fori_loop(0, n_chunks, step, 0)
        for cb in range(cb_per_iter * n_chunks, n_cb):
            project(cb)

    pl.when(i % 2 == 0)(lambda: run(z_even, z_odd))
    pl.when(i % 2 == 1)(lambda: run(z_odd, z_even))

    y0 = D_A + D_B
    for g in range(len(POOL_WINDOWS)):
        yc = jnp.dot(dbuf[:, slab(g)], pw_ref[g], preferred_element_type=F32) * prm("pool_scale", g)
        ybuf[:, y0 + g * POOL_GROUP:y0 + (g + 1) * POOL_GROUP] = yc.astype(BF16)

    o_ref[...] = hp_ref[...] + jnp.dot(ybuf[...], wout_ref[...], preferred_element_type=F32)
    normalize(hn_ref)

    for (buf, hist, width), out in zip(hist_bufs, (na_ref, nb_ref, np_ref)):
        for s in range(nseg):
            for c in range(width // LANE):
                out[s, :, slab(c)] = buf[s, c, L:L + hist, :]


def _mixer_params(w, layer, d):
    vectors = dict(mix_norm=w["mix_norm"][layer], conv_b_bias=w["conv_b_bias"][layer],
                   ln_b_gain=w["ln_b_gain"][layer], ln_b_bias=w["ln_b_bias"][layer],
                   pool_scale=w["pool_scale"][layer],
                   **{"conv_a_w%d" % k: w["conv_a_w"][layer, k] for k in range(K_A)})
    n_rows = 1 + max(row for row, _, _ in MIXER_PARAM_LAYOUT.values())
    rows = []
    for r in range(n_rows):
        placed = sorted((lane0, name) for name, (row, lane0, _) in MIXER_PARAM_LAYOUT.items() if row == r)
        parts, at = [], 0
        for lane0, name in placed:
            parts += [jnp.zeros((lane0 - at,), F32), vectors[name].astype(F32)]
            at = lane0 + MIXER_PARAM_LAYOUT[name][2]
        rows.append(jnp.concatenate(parts))
    return _pack_rows(rows, d)


def _mixer(h, layer, n_seq, pos0, caches, params, w_in, conv_b_w, pool_w, w_out, *, tm):
    m, d = h.shape
    seq_len = m // n_seq
    if seq_len >= tm:
        nseg, seg_len, tiles_per_seq = 1, tm, seq_len // tm
        assert seq_len % tm == 0
    else:
        nseg, seg_len, tiles_per_seq = tm // seq_len, seq_len, 1
        assert tm % seq_len == 0 and n_seq % nseg == 0
    assert seg_len % ROW_CHUNK == 0 and seg_len >= HIST_B
    has_cache = caches is not None
    d_in = w_in.shape[-1]
    n_tiles = m // tm
    assert d_in % COL_BLOCK == 0 and d_in // COL_BLOCK >= tm // ROW_CHUNK

    nxt = lambda i: jnp.minimum(i + 1, n_tiles - 1)
    prev = lambda i: jnp.maximum(i - 1, 0)

    def const(shape):
        nd = len(shape)
        return pl.BlockSpec((None,) + shape, lambda i: (layer,) + (0,) * nd, pipeline_mode=pl.Buffered(1))

    in_specs = [
        pl.BlockSpec((tm, d), lambda i: (nxt(i), 0)),
        pl.BlockSpec((tm, d), lambda i: (prev(i), 0)),
        const((d, d_in)),
        const((w_out.shape[1], d)),
        const((K_B, D_B)),
        const((len(POOL_WINDOWS), POOL_GROUP, POOL_GROUP)),
        pl.BlockSpec(params.shape, lambda i: (0, 0), pipeline_mode=pl.Buffered(1)),
    ]
    args = [h, h, w_in, w_out, conv_b_w, pool_w, params]
    if has_cache:
        ca, cb, cp = caches
        for c, hist in ((ca, HIST_A), (cb, HIST_B), (cp, HIST_C)):
            in_specs.append(pl.BlockSpec((None, nseg, hist, c.shape[-1]), lambda i: (layer, prev(i), 0, 0)))
            args.append(c)

    seq_of = lambda i: (prev(i) // tiles_per_seq, 0, 0)
    out_shape = [
        jax.ShapeDtypeStruct((m, d), F32),
        jax.ShapeDtypeStruct((n_seq, HIST_A, D_A), F32),
        jax.ShapeDtypeStruct((n_seq, HIST_B, D_B), F32),
        jax.ShapeDtypeStruct((n_seq, HIST_C, D_C), F32),
    ]
    out_specs = [
        pl.BlockSpec((tm, d), lambda i: (prev(i), 0)),
        pl.BlockSpec((nseg, HIST_A, D_A), seq_of),
        pl.BlockSpec((nseg, HIST_B, D_B), seq_of),
        pl.BlockSpec((nseg, HIST_C, D_C), seq_of),
    ]
    scratch = [
        pltpu.VMEM((nseg, D_A // LANE, HIST_A + seg_len, LANE), F32),
        pltpu.VMEM((nseg, D_B // LANE, HIST_B + seg_len, LANE), F32),
        pltpu.VMEM((nseg, D_C // LANE, HIST_C + seg_len, LANE), F32),
        pltpu.VMEM((d_in // COL_BLOCK, tm, COL_BLOCK), F32),
        pltpu.VMEM((d_in // COL_BLOCK, tm, COL_BLOCK), F32),
        pltpu.VMEM((tm, d), BF16),
        pltpu.VMEM((tm, d), BF16),
        pltpu.VMEM((tm, D_C), BF16),
    ]
    return pl.pallas_call(
        functools.partial(_mixer_body, nseg=nseg, seg_len=seg_len, tiles_per_seq=tiles_per_seq,
                          pos0=pos0, has_cache=has_cache),
        out_shape=out_shape,
        grid=(n_tiles + 1,),
        in_specs=in_specs,
        out_specs=out_specs,
        scratch_shapes=scratch,
        compiler_params=pltpu.CompilerParams(
            dimension_semantics=("arbitrary",),
            vmem_limit_bytes=VMEM_LIMIT_BYTES),
        name="mixer",
    )(*args)


def _front_pad(cache, hist):
    pad = hist - cache.shape[2]
    return jnp.pad(cache, ((0, 0), (0, 0), (pad, 0), (0, 0)))


def _trunk(x, pos0, caches, w, *, tm_ffn, tf, tm_mix):
    n_seq, seq_len, d = x.shape
    h = x.reshape(n_seq * seq_len, d)
    depth = w["w_in"].shape[0]
    new_a, new_b, new_p = [], [], []
    for l in range(depth):
        h = _ffn(h, w["ffn1_norm"], w["ffn1_wg"], w["ffn1_wu"], w["ffn1_wd"], l, None, tm=tm_ffn, tf=tf)
        h, na, nb, npool = _mixer(h, l, n_seq, pos0, caches, _mixer_params(w, l, d), w["w_in"],
                                  w["conv_b_w"], w["pool_w"], w["w_out"], tm=tm_mix)
        final_w = w["final_norm"] if l == depth - 1 else None
        h = _ffn(h, w["ffn2_norm"], w["ffn2_wg"], w["ffn2_wu"], w["ffn2_wd"], l, final_w, tm=tm_ffn, tf=tf)
        new_a.append(na[:, HIST_A - (K_A - 1):])
        new_b.append(nb[:, HIST_B - (K_B - 1):])
        new_p.append(npool[:, HIST_C - POOL_PAD:])
    return h.reshape(n_seq, seq_len, d), jnp.stack(new_a), jnp.stack(new_b), jnp.stack(new_p)


def kernel(x_prompt, x_sample, cache_conv_a, cache_conv_b, cache_pool, ffn1_norm, ffn1_wg, ffn1_wu, ffn1_wd,
           mix_norm, w_in, conv_a_w, conv_b_w, conv_b_bias, ln_b_gain, ln_b_bias, pool_w, pool_scale, w_out,
           ffn2_norm, ffn2_wg, ffn2_wu, ffn2_wd, final_norm):
    w = dict(
        ffn1_norm=ffn1_norm, ffn1_wg=ffn1_wg, ffn1_wu=ffn1_wu,
        ffn1_wd=ffn1_wd, mix_norm=mix_norm, w_in=w_in.astype(BF16), conv_a_w=conv_a_w,
        conv_b_w=conv_b_w, conv_b_bias=conv_b_bias, ln_b_gain=ln_b_gain, ln_b_bias=ln_b_bias,
        pool_w=pool_w.astype(BF16), pool_scale=pool_scale, w_out=w_out.astype(BF16),
        ffn2_norm=ffn2_norm, ffn2_wg=ffn2_wg, ffn2_wu=ffn2_wu,
        ffn2_wd=ffn2_wd, final_norm=final_norm)

    n_dec = x_sample.shape[0] * x_sample.shape[1]
    y_p, a_p, b_p, p_p = _trunk(x_prompt, 0, None, w, tm_ffn=1024, tf=256, tm_mix=256)
    caches = (_front_pad(cache_conv_a, HIST_A), _front_pad(cache_conv_b, HIST_B),
              _front_pad(cache_pool, HIST_C))
    y_s, a_s, b_s, p_s = _trunk(x_sample, PAST_LEN, caches, w, tm_ffn=n_dec, tf=512, tm_mix=256)
    return (y_p, y_s, a_p, b_p, p_p, a_s, b_s, p_s)
```

```python
import functools

import jax
import jax.numpy as jnp
from jax import lax
from jax.experimental import pallas as pl
from jax.experimental.pallas import tpu as pltpu

EPS = 1e-6
PAST_LEN = 1024
D_MODEL = 2048
D_A = 768
D_B = 768
D_C = 512
K_A = 3
K_B = 31
POOL_WINDOWS = (2, 4, 8, 16)
POOL_GROUP = D_C // len(POOL_WINDOWS)
POOL_PAD = max(POOL_WINDOWS) - 1

LANE = 128
SUBLANE = 8
COL_BLOCK = 256
HIST_A = 32
HIST_B = 32
HIST_C = 16
ROW_CHUNK = 32
FFN_TF = 512
MIXER_PARAM_LAYOUT = {
    "mix_norm": (0, 0, D_MODEL),
    "conv_a_w0": (1, 0, D_A), "conv_a_w1": (1, D_A, D_A),
    "conv_a_w2": (2, 0, D_A), "conv_b_bias": (2, D_A, D_B),
    "ln_b_gain": (3, 0, D_B), "ln_b_bias": (3, D_B, D_B),
    "pool_scale": (4, 0, D_C),
}

VMEM_LIMIT_BYTES = 60 * 1024 * 1024

F32 = jnp.float32
BF16 = jnp.bfloat16


def _rms_rows(x, w):
    ms = jnp.mean(x * x, axis=-1, keepdims=True)
    return (x * lax.rsqrt(ms + EPS)) * w


def _ffn_body(x_ref, wgu_ref, wd_ref, nw_ref, o_ref, xn_ref, *, final_norm):
    f = pl.program_id(1)
    tf = wgu_ref.shape[1] // 2

    @pl.when(f == 0)
    def _():
        x = x_ref[...]
        xn_ref[...] = _rms_rows(x, nw_ref[0:1, :]).astype(BF16)
        o_ref[...] = x

    gu = jnp.dot(xn_ref[...], wgu_ref[...], preferred_element_type=F32)
    g, u = gu[:, :tf], gu[:, tf:]
    a = ((0.5 * g) * jax.nn.sigmoid(g)) * u
    o_ref[...] += jnp.dot(a.astype(BF16), wd_ref[...].astype(BF16), preferred_element_type=F32)

    if final_norm:
        @pl.when(f == pl.num_programs(1) - 1)
        def _():
            o_ref[...] = _rms_rows(o_ref[...], nw_ref[1:2, :])


def _pack_rows(rows, width):
    rows = [jnp.pad(r.astype(F32), (0, width - r.shape[0])) for r in rows]
    rows += [jnp.zeros((width,), F32)] * (-len(rows) % SUBLANE)
    return jnp.stack(rows)


def _gate_up_blocks(wg, wu, tf):
    depth, d, d_ff = wg.shape
    chunks = lambda w: w.astype(BF16).reshape(depth, d, d_ff // tf, 1, tf)
    return jnp.concatenate([chunks(wg), chunks(wu)], axis=3).reshape(depth, d, 2 * d_ff)


def _ffn(x, norm_w, wgu, wd, layer, final_w, *, tm, tf):
    m, d = x.shape
    d_ff = wd.shape[1]
    assert m % tm == 0 and d_ff % tf == 0
    final_norm = final_w is not None
    nw = _pack_rows([norm_w[layer]] + ([final_w] if final_norm else []), d)
    return pl.pallas_call(
        functools.partial(_ffn_body, final_norm=final_norm),
        out_shape=jax.ShapeDtypeStruct((m, d), F32),
        grid=(m // tm, d_ff // tf),
        in_specs=[
            pl.BlockSpec((tm, d), lambda i, f: (i, 0)),
            pl.BlockSpec((None, d, 2 * tf), lambda i, f: (layer, 0, f)),
            pl.BlockSpec((None, tf, d), lambda i, f: (layer, f, 0)),
            pl.BlockSpec(nw.shape, lambda i, f: (0, 0)),
        ],
        out_specs=pl.BlockSpec((tm, d), lambda i, f: (i, 0)),
        scratch_shapes=[pltpu.VMEM((tm, d), BF16)],
        compiler_params=pltpu.CompilerParams(
            dimension_semantics=("arbitrary", "arbitrary"),
            vmem_limit_bytes=VMEM_LIMIT_BYTES),
        name="ffn",
    )(x, wgu, wd, nw)


def _mixer_body(*refs, nseg, seg_len, tiles_per_seq, pos0, has_cache):
    it = iter(refs)
    hn_ref, hp_ref, win_ref, wout_ref, cbw_ref, pw_ref, prm_ref = (next(it) for _ in range(7))
    if has_cache:
        ca_ref, cb_ref, cp_ref = (next(it) for _ in range(3))
    o_ref, na_ref, nb_ref, np_ref = (next(it) for _ in range(4))
    full_a, full_b, full_c, z_even, z_odd, xn_ref, ybuf, dbuf = (next(it) for _ in range(8))

    L = seg_len
    CH = ROW_CHUNK
    NG = CH // SUBLANE
    tm = nseg * L
    n_chunks = tm // CH
    chunks_per_seg = L // CH
    n_cb = win_ref.shape[1] // COL_BLOCK
    cb_per_iter = n_cb // n_chunks
    i = pl.program_id(0)
    je = (i + tiles_per_seq - 1) % tiles_per_seq
    slab = lambda c: slice(c * LANE, (c + 1) * LANE)

    def prm(name, c=None):
        row, lane0, width = MIXER_PARAM_LAYOUT[name]
        lanes = slice(lane0, lane0 + width) if c is None else slice(lane0 + c * LANE, lane0 + (c + 1) * LANE)
        return prm_ref[row:row + 1, lanes]

    conv_a_tap = lambda k, c: prm("conv_a_w%d" % k, c)
    conv_b_tap = lambda k, c: cbw_ref[k:k + 1, slab(c)]
    hist_bufs = ((full_a, HIST_A, D_A), (full_b, HIST_B, D_B), (full_c, HIST_C, D_C))

    def normalize(h_ref):
        xn_ref[...] = _rms_rows(h_ref[...], prm("mix_norm")).astype(BF16)

    @pl.when(i == 0)
    def _():
        z_odd[...] = jnp.zeros(z_odd.shape, F32)
        normalize(hp_ref)

    def init_history():
        cache_refs = (ca_ref, cb_ref, cp_ref) if has_cache else (None,) * 3
        for (buf, hist, width), cache in zip(hist_bufs, cache_refs):
            for s in range(nseg):
                for c in range(width // LANE):
                    buf[s, c, 0:hist, :] = cache[s, :, slab(c)] if has_cache else jnp.zeros((hist, LANE), F32)

    def carry_history():
        for buf, hist, width in hist_bufs:
            for s in range(nseg):
                for c in range(width // LANE):
                    buf[s, c, 0:hist, :] = buf[s, c, L:L + hist, :]

    if tiles_per_seq == 1:
        init_history()
    else:
        first = jnp.logical_or(je == 0, i == 0)
        pl.when(first)(init_history)
        pl.when(jnp.logical_not(first))(carry_history)

    def run(z_write, z_read):
        def project(cb, n=1):
            if isinstance(cb, int):
                cols = slice(cb * COL_BLOCK, (cb + n) * COL_BLOCK)
            else:
                cols = pl.ds(pl.multiple_of(cb * COL_BLOCK, n * COL_BLOCK), n * COL_BLOCK)
            zc = jnp.dot(xn_ref[...], win_ref[:, cols], preferred_element_type=F32)
            for b in range(n):
                z_write[cb + b] = zc[:, b * COL_BLOCK:(b + 1) * COL_BLOCK]

        def z(col, rows):
            return z_read[col // COL_BLOCK, rows, col % COL_BLOCK:col % COL_BLOCK + LANE]

        def tap(buf, s, c, row0):
            return buf[s, c, pl.ds(row0, SUBLANE, stride=1), :]

        def dwconv(buf, weight_tap, n_taps, hist, s, c, t0, newest):
            accs = [None] * NG
            for k in range(n_taps):
                wk = jnp.broadcast_to(weight_tap(k, c), (SUBLANE, LANE))
                for r in range(NG):
                    if k == n_taps - 1:
                        x = newest[r * SUBLANE:(r + 1) * SUBLANE]
                    else:
                        x = tap(buf, s, c, hist - (n_taps - 1) + k + t0 + r * SUBLANE)
                    accs[r] = wk * x if accs[r] is None else accs[r] + wk * x
            return jnp.concatenate(accs, axis=0)

        def mixers(s, t0, rows):
            for c in range(D_A // LANE):
                pa = z(2 * D_A + c * LANE, rows) * z(c * LANE, rows)
                full_a[s, c, pl.ds(HIST_A + t0, CH), :] = pa
                conv = dwconv(full_a, conv_a_tap, K_A, HIST_A, s, c, t0, pa)
                ybuf[rows, slab(c)] = (z(D_A + c * LANE, rows) * conv).astype(BF16)

            b0 = 3 * D_A
            xs = []
            for c in range(D_B // LANE):
                v = z(b0 + c * LANE, rows) * jax.nn.sigmoid(z(b0 + D_B + c * LANE, rows))
                full_b[s, c, pl.ds(HIST_B + t0, CH), :] = v
                xs.append(dwconv(full_b, conv_b_tap, K_B, HIST_B, s, c, t0, v) + prm("conv_b_bias", c))
            mu = jnp.sum(functools.reduce(jnp.add, xs), axis=-1, keepdims=True) * (1.0 / D_B)
            xc = [x - mu for x in xs]
            var = jnp.sum(functools.reduce(jnp.add, [x * x for x in xc]), axis=-1, keepdims=True) * (1.0 / D_B)
            inv = lax.rsqrt(var + EPS)
            for c in range(D_B // LANE):
                y = xc[c] * inv * prm("ln_b_gain", c) + prm("ln_b_bias", c)
                ybuf[rows, D_A + c * LANE:D_A + (c + 1) * LANE] = (y * jax.nn.sigmoid(y)).astype(BF16)

            c0 = 3 * D_A + 2 * D_B
            pos = pos0 + je * L + t0 + lax.broadcasted_iota(jnp.int32, (CH, POOL_GROUP), 0)
            for g, w in enumerate(POOL_WINDOWS):
                u = z(c0 + g * LANE, rows)
                full_c[s, g, pl.ds(HIST_C + t0, CH), :] = u
                accs = [u[r * SUBLANE:(r + 1) * SUBLANE] for r in range(NG)]
                for d in range(-(w - 1), (NG - 1) * SUBLANE):
                    users = [r for r in range(NG) if 1 <= r * SUBLANE - d <= w - 1]
                    if not users:
                        continue
                    if d >= 0 and d % SUBLANE == 0:
                        x = u[d:d + SUBLANE]
                    else:
                        x = tap(full_c, s, g, HIST_C + t0 + d)
                    for r in users:
                        accs[r] = accs[r] + x
                cnt = jnp.minimum(pos + 1, w).astype(F32)
                dbuf[rows, slab(g)] = (jnp.concatenate(accs, axis=0) / cnt - u).astype(BF16)

        def step(it_, carry):
            row0 = pl.multiple_of(it_ * CH, CH)
            if nseg == 1:
                s, t0 = 0, row0
            else:
                s = it_ // chunks_per_seg
                t0 = pl.multiple_of((it_ % chunks_per_seg) * CH, CH)
            mixers(s, t0, pl.ds(row0, CH))
            project(it_ * cb_per_iter, cb_per_iter)
            return carry

        lax.fori_loop(0, n_chunks, step, 0)
        for cb in range(cb_per_iter * n_chunks, n_cb):
            project(cb)

    pl.when(i % 2 == 0)(lambda: run(z_even, z_odd))
    pl.when(i % 2 == 1)(lambda: run(z_odd, z_even))

    y0 = D_A + D_B
    for g in range(len(POOL_WINDOWS)):
        yc = jnp.dot(dbuf[:, slab(g)], pw_ref[g], preferred_element_type=F32) * prm("pool_scale", g)
        ybuf[:, y0 + g * POOL_GROUP:y0 + (g + 1) * POOL_GROUP] = yc.astype(BF16)

    o_ref[...] = hp_ref[...] + jnp.dot(ybuf[...], wout_ref[...], preferred_element_type=F32)
    normalize(hn_ref)

    for (buf, hist, width), out in zip(hist_bufs, (na_ref, nb_ref, np_ref)):
        for s in range(nseg):
            for c in range(width // LANE):
                out[s, :, slab(c)] = buf[s, c, L:L + hist, :]


def _mixer_params(w, layer, d):
    vectors = dict(mix_norm=w["mix_norm"][layer], conv_b_bias=w["conv_b_bias"][layer],
                   ln_b_gain=w["ln_b_gain"][layer], ln_b_bias=w["ln_b_bias"][layer],
                   pool_scale=w["pool_scale"][layer],
                   **{"conv_a_w%d" % k: w["conv_a_w"][layer, k] for k in range(K_A)})
    n_rows = 1 + max(row for row, _, _ in MIXER_PARAM_LAYOUT.values())
    rows = []
    for r in range(n_rows):
        placed = sorted((lane0, name) for name, (row, lane0, _) in MIXER_PARAM_LAYOUT.items() if row == r)
        parts, at = [], 0
        for lane0, name in placed:
            parts += [jnp.zeros((lane0 - at,), F32), vectors[name].astype(F32)]
            at = lane0 + MIXER_PARAM_LAYOUT[name][2]
        rows.append(jnp.concatenate(parts))
    return _pack_rows(rows, d)


def _mixer(h, layer, n_seq, pos0, caches, params, w_in, conv_b_w, pool_w, w_out, *, tm):
    m, d = h.shape
    seq_len = m // n_seq
    if seq_len >= tm:
        nseg, seg_len, tiles_per_seq = 1, tm, seq_len // tm
        assert seq_len % tm == 0
    else:
        nseg, seg_len, tiles_per_seq = tm // seq_len, seq_len, 1
        assert tm % seq_len == 0 and n_seq % nseg == 0
    assert seg_len % ROW_CHUNK == 0 and seg_len >= HIST_B
    has_cache = caches is not None
    d_in = w_in.shape[-1]
    n_tiles = m // tm
    assert d_in % COL_BLOCK == 0 and d_in // COL_BLOCK >= tm // ROW_CHUNK

    nxt = lambda i: jnp.minimum(i + 1, n_tiles - 1)
    prev = lambda i: jnp.maximum(i - 1, 0)

    def const(shape):
        nd = len(shape)
        return pl.BlockSpec((None,) + shape, lambda i: (layer,) + (0,) * nd, pipeline_mode=pl.Buffered(1))

    in_specs = [
        pl.BlockSpec((tm, d), lambda i: (nxt(i), 0)),
        pl.BlockSpec((tm, d), lambda i: (prev(i), 0)),
        const((d, d_in)),
        const((w_out.shape[1], d)),
        const((K_B, D_B)),
        const((len(POOL_WINDOWS), POOL_GROUP, POOL_GROUP)),
        pl.BlockSpec(params.shape, lambda i: (0, 0), pipeline_mode=pl.Buffered(1)),
    ]
    args = [h, h, w_in, w_out, conv_b_w, pool_w, params]
    if has_cache:
        ca, cb, cp = caches
        for c, hist in ((ca, HIST_A), (cb, HIST_B), (cp, HIST_C)):
            in_specs.append(pl.BlockSpec((None, nseg, hist, c.shape[-1]), lambda i: (layer, prev(i), 0, 0)))
            args.append(c)

    seq_of = lambda i: (prev(i) // tiles_per_seq, 0, 0)
    out_shape = [
        jax.ShapeDtypeStruct((m, d), F32),
        jax.ShapeDtypeStruct((n_seq, HIST_A, D_A), F32),
        jax.ShapeDtypeStruct((n_seq, HIST_B, D_B), F32),
        jax.ShapeDtypeStruct((n_seq, HIST_C, D_C), F32),
    ]
    out_specs = [
        pl.BlockSpec((tm, d), lambda i: (prev(i), 0)),
        pl.BlockSpec((nseg, HIST_A, D_A), seq_of),
        pl.BlockSpec((nseg, HIST_B, D_B), seq_of),
        pl.BlockSpec((nseg, HIST_C, D_C), seq_of),
    ]
    scratch = [
        pltpu.VMEM((nseg, D_A // LANE, HIST_A + seg_len, LANE), F32),
        pltpu.VMEM((nseg, D_B // LANE, HIST_B + seg_len, LANE), F32),
        pltpu.VMEM((nseg, D_C // LANE, HIST_C + seg_len, LANE), F32),
        pltpu.VMEM((d_in // COL_BLOCK, tm, COL_BLOCK), F32),
        pltpu.VMEM((d_in // COL_BLOCK, tm, COL_BLOCK), F32),
        pltpu.VMEM((tm, d), BF16),
        pltpu.VMEM((tm, d), BF16),
        pltpu.VMEM((tm, D_C), BF16),
    ]
    return pl.pallas_call(
        functools.partial(_mixer_body, nseg=nseg, seg_len=seg_len, tiles_per_seq=tiles_per_seq,
                          pos0=pos0, has_cache=has_cache),
        out_shape=out_shape,
        grid=(n_tiles + 1,),
        in_specs=in_specs,
        out_specs=out_specs,
        scratch_shapes=scratch,
        compiler_params=pltpu.CompilerParams(
            dimension_semantics=("arbitrary",),
            vmem_limit_bytes=VMEM_LIMIT_BYTES),
        name="mixer",
    )(*args)


def _front_pad(cache, hist):
    pad = hist - cache.shape[2]
    return jnp.pad(cache, ((0, 0), (0, 0), (pad, 0), (0, 0)))


def _trunk(x, pos0, caches, w, *, tm_ffn, tf, tm_mix):
    n_seq, seq_len, d = x.shape
    h = x.reshape(n_seq * seq_len, d)
    depth = w["w_in"].shape[0]
    new_a, new_b, new_p = [], [], []
    for l in range(depth):
        h = _ffn(h, w["ffn1_norm"], w["ffn1_wgu"], w["ffn1_wd"], l, None, tm=tm_ffn, tf=tf)
        h, na, nb, npool = _mixer(h, l, n_seq, pos0, caches, _mixer_params(w, l, d), w["w_in"],
                                  w["conv_b_w"], w["pool_w"], w["w_out"], tm=tm_mix)
        final_w = w["final_norm"] if l == depth - 1 else None
        h = _ffn(h, w["ffn2_norm"], w["ffn2_wgu"], w["ffn2_wd"], l, final_w, tm=tm_ffn, tf=tf)
        new_a.append(na[:, HIST_A - (K_A - 1):])
        new_b.append(nb[:, HIST_B - (K_B - 1):])
        new_p.append(npool[:, HIST_C - POOL_PAD:])
    return h.reshape(n_seq, seq_len, d), jnp.stack(new_a), jnp.stack(new_b), jnp.stack(new_p)


def kernel(x_prompt, x_sample, cache_conv_a, cache_conv_b, cache_pool, ffn1_norm, ffn1_wg, ffn1_wu, ffn1_wd,
           mix_norm, w_in, conv_a_w, conv_b_w, conv_b_bias, ln_b_gain, ln_b_bias, pool_w, pool_scale, w_out,
           ffn2_norm, ffn2_wg, ffn2_wu, ffn2_wd, final_norm):
    w = dict(
        ffn1_norm=ffn1_norm, ffn1_wgu=_gate_up_blocks(ffn1_wg, ffn1_wu, FFN_TF),
        ffn1_wd=ffn1_wd, mix_norm=mix_norm, w_in=w_in.astype(BF16), conv_a_w=conv_a_w,
        conv_b_w=conv_b_w, conv_b_bias=conv_b_bias, ln_b_gain=ln_b_gain, ln_b_bias=ln_b_bias,
        pool_w=pool_w.astype(BF16), pool_scale=pool_scale, w_out=w_out.astype(BF16),
        ffn2_norm=ffn2_norm, ffn2_wgu=_gate_up_blocks(ffn2_wg, ffn2_wu, FFN_TF),
        ffn2_wd=ffn2_wd, final_norm=final_norm)

    n_dec = x_sample.shape[0] * x_sample.shape[1]
    y_p, a_p, b_p, p_p = _trunk(x_prompt, 0, None, w, tm_ffn=1024, tf=FFN_TF, tm_mix=256)
    caches = (_front_pad(cache_conv_a, HIST_A), _front_pad(cache_conv_b, HIST_B),
              _front_pad(cache_pool, HIST_C))
    y_s, a_s, b_s, p_s = _trunk(x_sample, PAST_LEN, caches, w, tm_ffn=n_dec, tf=FFN_TF, tm_mix=256)
    return (y_p, y_s, a_p, b_p, p_p, a_s, b_s, p_s)
```

```python
import functools

import jax
import jax.numpy as jnp
from jax import lax
from jax.experimental import pallas as pl
from jax.experimental.pallas import tpu as pltpu

EPS = 1e-6
PAST_LEN = 1024
D_MODEL = 2048
D_A = 768
D_B = 768
D_C = 512
K_A = 3
K_B = 31
POOL_WINDOWS = (2, 4, 8, 16)
POOL_GROUP = D_C // len(POOL_WINDOWS)
POOL_PAD = max(POOL_WINDOWS) - 1

LANE = 128
SUBLANE = 8
COL_BLOCK = 256
HIST_A = 32
HIST_B = 32
HIST_C = 16
ROW_CHUNK = 32
MIXER_PARAM_LAYOUT = {
    "mix_norm": (0, 0, D_MODEL),
    "conv_a_w0": (1, 0, D_A), "conv_a_w1": (1, D_A, D_A),
    "conv_a_w2": (2, 0, D_A), "conv_b_bias": (2, D_A, D_B),
    "ln_b_gain": (3, 0, D_B), "ln_b_bias": (3, D_B, D_B),
    "pool_scale": (4, 0, D_C),
}

VMEM_LIMIT_BYTES = 60 * 1024 * 1024

F32 = jnp.float32
BF16 = jnp.bfloat16


def _rms_rows(x, w):
    ms = jnp.mean(x * x, axis=-1, keepdims=True)
    return (x * lax.rsqrt(ms + EPS)) * w


def _ffn_body(x_ref, wg_ref, wu_ref, wd_ref, nw_ref, o_ref, xn_ref, *, final_norm):
    f = pl.program_id(1)

    @pl.when(f == 0)
    def _():
        x = x_ref[...]
        xn_ref[...] = _rms_rows(x, nw_ref[0:1, :]).astype(BF16)
        o_ref[...] = x

    xn = xn_ref[...]
    g = jnp.dot(xn, wg_ref[...], preferred_element_type=F32)
    u = jnp.dot(xn, wu_ref[...], preferred_element_type=F32)
    a = ((0.5 * g) * jax.nn.sigmoid(g)) * u
    o_ref[...] += jnp.dot(a.astype(BF16), wd_ref[...].astype(BF16), preferred_element_type=F32)

    if final_norm:
        @pl.when(f == pl.num_programs(1) - 1)
        def _():
            o_ref[...] = _rms_rows(o_ref[...], nw_ref[1:2, :])


def _pack_rows(rows, width):
    rows = [jnp.pad(r.astype(F32), (0, width - r.shape[0])) for r in rows]
    rows += [jnp.zeros((width,), F32)] * (-len(rows) % SUBLANE)
    return jnp.stack(rows)


def _ffn(x, norm_w, wg, wu, wd, layer, final_w, *, tm, tf):
    m, d = x.shape
    d_ff = wg.shape[-1]
    assert m % tm == 0 and d_ff % tf == 0
    final_norm = final_w is not None
    nw = _pack_rows([norm_w[layer]] + ([final_w] if final_norm else []), d)
    return pl.pallas_call(
        functools.partial(_ffn_body, final_norm=final_norm),
        out_shape=jax.ShapeDtypeStruct((m, d), F32),
        grid=(m // tm, d_ff // tf),
        in_specs=[
            pl.BlockSpec((tm, d), lambda i, f: (i, 0)),
            pl.BlockSpec((None, d, tf), lambda i, f: (layer, 0, f)),
            pl.BlockSpec((None, d, tf), lambda i, f: (layer, 0, f)),
            pl.BlockSpec((None, tf, d), lambda i, f: (layer, f, 0)),
            pl.BlockSpec(nw.shape, lambda i, f: (0, 0)),
        ],
        out_specs=pl.BlockSpec((tm, d), lambda i, f: (i, 0)),
        scratch_shapes=[pltpu.VMEM((tm, d), BF16)],
        compiler_params=pltpu.CompilerParams(
            dimension_semantics=("arbitrary", "arbitrary"),
            vmem_limit_bytes=VMEM_LIMIT_BYTES),
        name="ffn",
    )(x, wg, wu, wd, nw)


def _mixer_body(*refs, nseg, seg_len, tiles_per_seq, n_tiles, pos0, has_cache):
    it = iter(refs)
    hn_ref, hp_ref, win_ref, wout_ref, cbw_ref, pw_ref, prm_ref = (next(it) for _ in range(7))
    if has_cache:
        ca_ref, cb_ref, cp_ref = (next(it) for _ in range(3))
    o_ref, na_ref, nb_ref, np_ref = (next(it) for _ in range(4))
    full_a, full_b, full_c, z_even, z_odd, xn_ref, ybuf, dbuf = (next(it) for _ in range(8))

    L = seg_len
    CH = ROW_CHUNK
    NG = CH // SUBLANE
    tm = nseg * L
    n_chunks = tm // CH
    chunks_per_seg = L // CH
    n_cb = win_ref.shape[1] // COL_BLOCK
    cb_per_iter = n_cb // n_chunks
    i = pl.program_id(0)
    je = (i + tiles_per_seq - 1) % tiles_per_seq
    slab = lambda c: slice(c * LANE, (c + 1) * LANE)

    def prm(name, c=None):
        row, lane0, width = MIXER_PARAM_LAYOUT[name]
        lanes = slice(lane0, lane0 + width) if c is None else slice(lane0 + c * LANE, lane0 + (c + 1) * LANE)
        return prm_ref[row:row + 1, lanes]

    conv_a_tap = lambda k, c: prm("conv_a_w%d" % k, c)
    conv_b_tap = lambda k, c: cbw_ref[k:k + 1, slab(c)]
    hist_bufs = ((full_a, HIST_A, D_A), (full_b, HIST_B, D_B), (full_c, HIST_C, D_C))

    def normalize(h_ref):
        xn_ref[...] = _rms_rows(h_ref[...], prm("mix_norm")).astype(BF16)

    pl.when(i == 0)(lambda: normalize(hp_ref))

    def init_history():
        cache_refs = (ca_ref, cb_ref, cp_ref) if has_cache else (None,) * 3
        for (buf, hist, width), cache in zip(hist_bufs, cache_refs):
            for s in range(nseg):
                for c in range(width // LANE):
                    buf[s, c, 0:hist, :] = cache[s, :, slab(c)] if has_cache else jnp.zeros((hist, LANE), F32)

    def carry_history():
        for buf, hist, width in hist_bufs:
            for s in range(nseg):
                for c in range(width // LANE):
                    buf[s, c, 0:hist, :] = buf[s, c, L:L + hist, :]

    if tiles_per_seq == 1:
        init_history()
    else:
        first = jnp.logical_or(je == 0, i == 0)
        pl.when(first)(init_history)
        pl.when(jnp.logical_not(first))(carry_history)

    def run(z_write, z_read):
        def project(cb, n=1):
            if isinstance(cb, int):
                cols = slice(cb * COL_BLOCK, (cb + n) * COL_BLOCK)
            else:
                cols = pl.ds(pl.multiple_of(cb * COL_BLOCK, n * COL_BLOCK), n * COL_BLOCK)
            zc = jnp.dot(xn_ref[...], win_ref[:, cols], preferred_element_type=F32)
            for b in range(n):
                z_write[cb + b] = zc[:, b * COL_BLOCK:(b + 1) * COL_BLOCK]

        def z(col, rows):
            return z_read[col // COL_BLOCK, rows, col % COL_BLOCK:col % COL_BLOCK + LANE]

        def tap(buf, s, c, row0):
            return buf[s, c, pl.ds(row0, SUBLANE, stride=1), :]

        def dwconv(buf, weight_tap, n_taps, hist, s, c, t0, newest):
            accs = [None] * NG
            for k in range(n_taps):
                wk = jnp.broadcast_to(weight_tap(k, c), (SUBLANE, LANE))
                for r in range(NG):
                    if k == n_taps - 1:
                        x = newest[r * SUBLANE:(r + 1) * SUBLANE]
                    else:
                        x = tap(buf, s, c, hist - (n_taps - 1) + k + t0 + r * SUBLANE)
                    accs[r] = wk * x if accs[r] is None else accs[r] + wk * x
            return jnp.concatenate(accs, axis=0)

        def mixers(s, t0, rows):
            for c in range(D_A // LANE):
                pa = z(2 * D_A + c * LANE, rows) * z(c * LANE, rows)
                full_a[s, c, pl.ds(HIST_A + t0, CH), :] = pa
                conv = dwconv(full_a, conv_a_tap, K_A, HIST_A, s, c, t0, pa)
                ybuf[rows, slab(c)] = (z(D_A + c * LANE, rows) * conv).astype(BF16)

            b0 = 3 * D_A
            xs = []
            for c in range(D_B // LANE):
                v = z(b0 + c * LANE, rows) * jax.nn.sigmoid(z(b0 + D_B + c * LANE, rows))
                full_b[s, c, pl.ds(HIST_B + t0, CH), :] = v
                xs.append(dwconv(full_b, conv_b_tap, K_B, HIST_B, s, c, t0, v) + prm("conv_b_bias", c))
            mu = jnp.sum(functools.reduce(jnp.add, xs), axis=-1, keepdims=True) * (1.0 / D_B)
            xc = [x - mu for x in xs]
            var = jnp.sum(functools.reduce(jnp.add, [x * x for x in xc]), axis=-1, keepdims=True) * (1.0 / D_B)
            inv = lax.rsqrt(var + EPS)
            for c in range(D_B // LANE):
                y = xc[c] * inv * prm("ln_b_gain", c) + prm("ln_b_bias", c)
                ybuf[rows, D_A + c * LANE:D_A + (c + 1) * LANE] = (y * jax.nn.sigmoid(y)).astype(BF16)

            c0 = 3 * D_A + 2 * D_B
            pos = pos0 + je * L + t0 + lax.broadcasted_iota(jnp.int32, (CH, POOL_GROUP), 0)
            for g, w in enumerate(POOL_WINDOWS):
                u = z(c0 + g * LANE, rows)
                full_c[s, g, pl.ds(HIST_C + t0, CH), :] = u
                accs = [u[r * SUBLANE:(r + 1) * SUBLANE] for r in range(NG)]
                for d in range(-(w - 1), (NG - 1) * SUBLANE):
                    users = [r for r in range(NG) if 1 <= r * SUBLANE - d <= w - 1]
                    if not users:
                        continue
                    if d >= 0 and d % SUBLANE == 0:
                        x = u[d:d + SUBLANE]
                    else:
                        x = tap(full_c, s, g, HIST_C + t0 + d)
                    for r in users:
                        accs[r] = accs[r] + x
                cnt = jnp.minimum(pos + 1, w).astype(F32)
                dbuf[rows, slab(g)] = (jnp.concatenate(accs, axis=0) / cnt - u).astype(BF16)

        def step(it_, carry):
            row0 = pl.multiple_of(it_ * CH, CH)
            if nseg == 1:
                s, t0 = 0, row0
            else:
                s = it_ // chunks_per_seg
                t0 = pl.multiple_of((it_ % chunks_per_seg) * CH, CH)
            if z_read is not None:
                mixers(s, t0, pl.ds(row0, CH))
            if z_write is not None:
                project(it_ * cb_per_iter, cb_per_iter)
            return carry

        lax.fori_loop(0, n_chunks, step, 0)
        if z_write is not None:
            for cb in range(cb_per_iter * n_chunks, n_cb):
                project(cb)

    z_last = z_odd if (n_tiles - 1) % 2 else z_even
    interior = jnp.logical_and(i > 0, i < n_tiles)
    pl.when(i == 0)(lambda: run(z_even, None))
    pl.when(jnp.logical_and(interior, i % 2 == 0))(lambda: run(z_even, z_odd))
    pl.when(jnp.logical_and(interior, i % 2 == 1))(lambda: run(z_odd, z_even))
    pl.when(i == n_tiles)(lambda: run(None, z_last))

    @pl.when(i > 0)
    def _():
        y0 = D_A + D_B
        for g in range(len(POOL_WINDOWS)):
            yc = jnp.dot(dbuf[:, slab(g)], pw_ref[g], preferred_element_type=F32) * prm("pool_scale", g)
            ybuf[:, y0 + g * POOL_GROUP:y0 + (g + 1) * POOL_GROUP] = yc.astype(BF16)

        o_ref[...] = hp_ref[...] + jnp.dot(ybuf[...], wout_ref[...], preferred_element_type=F32)

        for (buf, hist, width), out in zip(hist_bufs, (na_ref, nb_ref, np_ref)):
            for s in range(nseg):
                for c in range(width // LANE):
                    out[s, :, slab(c)] = buf[s, c, L:L + hist, :]

    pl.when(i < n_tiles)(lambda: normalize(hn_ref))


def _mixer_params(w, layer, d):
    vectors = dict(mix_norm=w["mix_norm"][layer], conv_b_bias=w["conv_b_bias"][layer],
                   ln_b_gain=w["ln_b_gain"][layer], ln_b_bias=w["ln_b_bias"][layer],
                   pool_scale=w["pool_scale"][layer],
                   **{"conv_a_w%d" % k: w["conv_a_w"][layer, k] for k in range(K_A)})
    n_rows = 1 + max(row for row, _, _ in MIXER_PARAM_LAYOUT.values())
    rows = []
    for r in range(n_rows):
        placed = sorted((lane0, name) for name, (row, lane0, _) in MIXER_PARAM_LAYOUT.items() if row == r)
        parts, at = [], 0
        for lane0, name in placed:
            parts += [jnp.zeros((lane0 - at,), F32), vectors[name].astype(F32)]
            at = lane0 + MIXER_PARAM_LAYOUT[name][2]
        rows.append(jnp.concatenate(parts))
    return _pack_rows(rows, d)


def _mixer(h, layer, n_seq, pos0, caches, params, w_in, conv_b_w, pool_w, w_out, *, tm):
    m, d = h.shape
    seq_len = m // n_seq
    if seq_len >= tm:
        nseg, seg_len, tiles_per_seq = 1, tm, seq_len // tm
        assert seq_len % tm == 0
    else:
        nseg, seg_len, tiles_per_seq = tm // seq_len, seq_len, 1
        assert tm % seq_len == 0 and n_seq % nseg == 0
    assert seg_len % ROW_CHUNK == 0 and seg_len >= HIST_B
    has_cache = caches is not None
    d_in = w_in.shape[-1]
    n_tiles = m // tm
    assert d_in % COL_BLOCK == 0 and d_in // COL_BLOCK >= tm // ROW_CHUNK

    nxt = lambda i: jnp.minimum(i + 1, n_tiles - 1)
    prev = lambda i: jnp.maximum(i - 1, 0)

    def const(shape):
        nd = len(shape)
        return pl.BlockSpec((None,) + shape, lambda i: (layer,) + (0,) * nd, pipeline_mode=pl.Buffered(1))

    in_specs = [
        pl.BlockSpec((tm, d), lambda i: (nxt(i), 0)),
        pl.BlockSpec((tm, d), lambda i: (prev(i), 0)),
        const((d, d_in)),
        const((w_out.shape[1], d)),
        const((K_B, D_B)),
        const((len(POOL_WINDOWS), POOL_GROUP, POOL_GROUP)),
        pl.BlockSpec(params.shape, lambda i: (0, 0), pipeline_mode=pl.Buffered(1)),
    ]
    args = [h, h, w_in, w_out, conv_b_w, pool_w, params]
    if has_cache:
        ca, cb, cp = caches
        for c, hist in ((ca, HIST_A), (cb, HIST_B), (cp, HIST_C)):
            in_specs.append(pl.BlockSpec((None, nseg, hist, c.shape[-1]), lambda i: (layer, prev(i), 0, 0)))
            args.append(c)

    seq_of = lambda i: (prev(i) // tiles_per_seq, 0, 0)
    out_shape = [
        jax.ShapeDtypeStruct((m, d), F32),
        jax.ShapeDtypeStruct((n_seq, HIST_A, D_A), F32),
        jax.ShapeDtypeStruct((n_seq, HIST_B, D_B), F32),
        jax.ShapeDtypeStruct((n_seq, HIST_C, D_C), F32),
    ]
    out_specs = [
        pl.BlockSpec((tm, d), lambda i: (prev(i), 0)),
        pl.BlockSpec((nseg, HIST_A, D_A), seq_of),
        pl.BlockSpec((nseg, HIST_B, D_B), seq_of),
        pl.BlockSpec((nseg, HIST_C, D_C), seq_of),
    ]
    scratch = [
        pltpu.VMEM((nseg, D_A // LANE, HIST_A + seg_len, LANE), F32),
        pltpu.VMEM((nseg, D_B // LANE, HIST_B + seg_len, LANE), F32),
        pltpu.VMEM((nseg, D_C // LANE, HIST_C + seg_len, LANE), F32),
        pltpu.VMEM((d_in // COL_BLOCK, tm, COL_BLOCK), F32),
        pltpu.VMEM((d_in // COL_BLOCK, tm, COL_BLOCK), F32),
        pltpu.VMEM((tm, d), BF16),
        pltpu.VMEM((tm, d), BF16),
        pltpu.VMEM((tm, D_C), BF16),
    ]
    return pl.pallas_call(
        functools.partial(_mixer_body, nseg=nseg, seg_len=seg_len, tiles_per_seq=tiles_per_seq,
                          n_tiles=n_tiles, pos0=pos0, has_cache=has_cache),
        out_shape=out_shape,
        grid=(n_tiles + 1,),
        in_specs=in_specs,
        out_specs=out_specs,
        scratch_shapes=scratch,
        compiler_params=pltpu.CompilerParams(
            dimension_semantics=("arbitrary",),
            vmem_limit_bytes=VMEM_LIMIT_BYTES),
        name="mixer",
    )(*args)


def _front_pad(cache, hist):
    pad = hist - cache.shape[2]
    return jnp.pad(cache, ((0, 0), (0, 0), (pad, 0), (0, 0)))


def _trunk(x, pos0, caches, w, *, tm_ffn, tf, tm_mix):
    n_seq, seq_len, d = x.shape
    h = x.reshape(n_seq * seq_len, d)
    depth = w["w_in"].shape[0]
    new_a, new_b, new_p = [], [], []
    for l in range(depth):
        h = _ffn(h, w["ffn1_norm"], w["ffn1_wg"], w["ffn1_wu"], w["ffn1_wd"], l, None, tm=tm_ffn, tf=tf)
        h, na, nb, npool = _mixer(h, l, n_seq, pos0, caches, _mixer_params(w, l, d), w["w_in"],
                                  w["conv_b_w"], w["pool_w"], w["w_out"], tm=tm_mix)
        final_w = w["final_norm"] if l == depth - 1 else None
        h = _ffn(h, w["ffn2_norm"], w["ffn2_wg"], w["ffn2_wu"], w["ffn2_wd"], l, final_w, tm=tm_ffn, tf=tf)
        new_a.append(na[:, HIST_A - (K_A - 1):])
        new_b.append(nb[:, HIST_B - (K_B - 1):])
        new_p.append(npool[:, HIST_C - POOL_PAD:])
    return h.reshape(n_seq, seq_len, d), jnp.stack(new_a), jnp.stack(new_b), jnp.stack(new_p)


def kernel(x_prompt, x_sample, cache_conv_a, cache_conv_b, cache_pool, ffn1_norm, ffn1_wg, ffn1_wu, ffn1_wd,
           mix_norm, w_in, conv_a_w, conv_b_w, conv_b_bias, ln_b_gain, ln_b_bias, pool_w, pool_scale, w_out,
           ffn2_norm, ffn2_wg, ffn2_wu, ffn2_wd, final_norm):
    w = dict(
        ffn1_norm=ffn1_norm, ffn1_wg=ffn1_wg.astype(BF16), ffn1_wu=ffn1_wu.astype(BF16),
        ffn1_wd=ffn1_wd, mix_norm=mix_norm, w_in=w_in.astype(BF16), conv_a_w=conv_a_w,
        conv_b_w=conv_b_w, conv_b_bias=conv_b_bias, ln_b_gain=ln_b_gain, ln_b_bias=ln_b_bias,
        pool_w=pool_w.astype(BF16), pool_scale=pool_scale, w_out=w_out.astype(BF16),
        ffn2_norm=ffn2_norm, ffn2_wg=ffn2_wg.astype(BF16), ffn2_wu=ffn2_wu.astype(BF16),
        ffn2_wd=ffn2_wd, final_norm=final_norm)

    n_dec = x_sample.shape[0] * x_sample.shape[1]
    y_p, a_p, b_p, p_p = _trunk(x_prompt, 0, None, w, tm_ffn=1024, tf=512, tm_mix=256)
    caches = (_front_pad(cache_conv_a, HIST_A), _front_pad(cache_conv_b, HIST_B),
              _front_pad(cache_pool, HIST_C))
    y_s, a_s, b_s, p_s = _trunk(x_sample, PAST_LEN, caches, w, tm_ffn=n_dec, tf=512, tm_mix=256)
    return (y_p, y_s, a_p, b_p, p_p, a_s, b_s, p_s)
```
